```python
import jax, jax.numpy as jnp
from jax import lax
import numpy as np

D_MODEL = 1024
BATCH = 2
SEQ = 8192
DEPTH = 4
DEC_BATCH = 16
DEC_SEQ = 2048
PAST_LEN = 128

GRID_W = 64
N_MIXERS = 2
N_HEADS = 16
HEAD_DIM = D_MODEL // N_HEADS
WIN_H = 8
WIN_W = 16
Q_COL_BLOCK = 16
K_COL_SPAN = Q_COL_BLOCK + WIN_W
N_COL_BLOCKS = GRID_W // Q_COL_BLOCK
RPB_H = 2 * WIN_H - 1
RPB_W = 2 * WIN_W - 1
D_FF = 2816
CONV_W = 3
LORA_DECAY = 64
LORA_ICLR = 64
LORA_VALUE = 32
LORA_GATE = 160
N_A_LAYERS = (DEPTH + 1) // 2
N_B_LAYERS = DEPTH // 2
RMS_EPS = 1e-6
GN_EPS = 64e-5
NEG_INF = -1e30

kernel_name = "hybrid_natten_rwkv7_convffn_encoder"


def _rms(x, g):
    xf = x.astype(jnp.float32)
    y = xf * lax.rsqrt(jnp.mean(xf * xf, axis=-1, keepdims=True) + RMS_EPS)
    return (y * g.astype(jnp.float32)).astype(x.dtype)


def _na_indices(rows):
    kh = min(WIN_H, rows)
    r = np.arange(rows)
    rs = np.clip(r - kh // 2, 0, rows - kh)
    row_idx = rs[:, None] + np.arange(kh)[None, :]
    jb = np.arange(N_COL_BLOCKS)
    cb = np.clip(jb * Q_COL_BLOCK - WIN_W // 2, 0, GRID_W - K_COL_SPAN)
    col_idx = cb[:, None] + np.arange(K_COL_SPAN)[None, :]
    c = np.arange(GRID_W).reshape(N_COL_BLOCKS, Q_COL_BLOCK)
    cs = np.clip(c - WIN_W // 2, 0, GRID_W - WIN_W)
    kc = col_idx[:, None, :]
    col_ok = (kc >= cs[..., None]) & (kc < cs[..., None] + WIN_W)
    d_row = row_idx - r[:, None] + WIN_H - 1
    d_col = np.clip(kc - c[..., None] + WIN_W - 1, 0, RPB_W - 1)
    return row_idx, col_idx, col_ok, d_row, d_col


def neighbourhood_attention(h, w_qkv, w_o, q_gain, k_gain, rpb):
    b, s, d = h.shape
    rows = s // GRID_W
    row_idx, col_idx, col_ok, d_row, d_col = _na_indices(rows)
    qkv = (h @ w_qkv).reshape(b, rows, GRID_W, 3, N_HEADS, HEAD_DIM)
    q = _rms(qkv[:, :, :, 0], q_gain) * (HEAD_DIM ** -0.5)
    k = _rms(qkv[:, :, :, 1], k_gain)
    v = qkv[:, :, :, 2]
    q = q.reshape(b, rows, N_COL_BLOCKS, Q_COL_BLOCK, N_HEADS, HEAD_DIM)
    ri = row_idx[:, :, None, None]
    ci = col_idx[None, None, :, :]
    kb = k[:, ri, ci]
    vb = v[:, ri, ci]
    sc = jnp.einsum('brjqhd,brijchd->brjhqic', q, kb, preferred_element_type=jnp.float32)
    bias = rpb.astype(jnp.float32)[:, d_row[:, None, None, :, None], d_col[None, :, :, None, :]]
    bias = jnp.transpose(bias, (1, 2, 0, 3, 4, 5))
    mask = col_ok[:, None, :, None, :]
    sc = jnp.where(mask, sc + bias, NEG_INF)
    p = jax.nn.softmax(sc, axis=(-2, -1)).astype(v.dtype)
    o = jnp.einsum('brjhqic,brijchd->brjqhd', p, vb)
    return o.reshape(b, s, d) @ w_o


def _centred_shift(x):
    prev = jnp.pad(x[:, :-1], ((0, 0), (1, 0), (0, 0)))
    nxt = jnp.pad(x[:, 1:], ((0, 0), (0, 1), (0, 0)))
    return 0.5 * (prev + nxt) - x


def _heads(z):
    return z.reshape(z.shape[:-1] + (N_HEADS, HEAD_DIM))


def _flip_bwd(z):
    return jnp.stack([z[0], jnp.flip(z[1], axis=1)])


def _wkv7_step(S, inp):
    r_t, w_t, k_t, v_t, aa_t, bb_t = inp
    sa = jnp.einsum('zbhij,zbhj->zbhi', S, aa_t)
    S = S * w_t[..., None, :] + sa[..., :, None] * bb_t[..., None, :] + v_t[..., :, None] * k_t[..., None, :]
    y = jnp.einsum('zbhij,zbhj->zbhi', S, r_t)
    return S, y


def rwkv7_mixer(h, v_first, vres, mu, w_r, w_k, w_v, w_o, w0, w1, w2, a0, a1, a2, g1, g2, k_k, k_a, r_k, ln_w, ln_b):
    b, t, d = h.shape
    f32 = jnp.float32
    xx = _centred_shift(h)
    xr, xw, xk, xv, xa, xg = (h + xx * mu[i] for i in range(6))
    r = xr @ w_r
    k = xk @ w_k
    v = xv @ w_v
    if vres is None:
        v_first = v
    else:
        v0, v1, v2 = vres
        v = v + (v_first - v) * jax.nn.sigmoid(v0 + (xv @ v1) @ v2)
    g = jax.nn.sigmoid(xg @ g1) @ g2
    lw = jnp.einsum('zbtl,zld->zbtd', jnp.tanh(jnp.einsum('btd,zdl->zbtl', xw, w1)), w2)
    w_raw = -jax.nn.softplus(-(w0[:, None, None].astype(f32) + lw.astype(f32))) - 0.5
    decay = jnp.exp(-jnp.exp(w_raw))
    la = jnp.einsum('zbtl,zld->zbtd', jnp.einsum('btd,zdl->zbtl', xa, a1), a2)
    a = jax.nn.sigmoid(a0[:, None, None].astype(f32) + la.astype(f32))
    kk = _heads((k * k_k).astype(f32))
    kk = kk / jnp.maximum(jnp.sqrt(jnp.sum(kk * kk, axis=-1, keepdims=True)), 1e-12)
    k_dir = k.astype(f32)[None] * (1.0 + (a - 1.0) * k_a.astype(f32))
    a_h, k_h, w_h = _heads(a), _heads(k_dir), _heads(decay)
    r_h, v_h = _heads(r.astype(f32)), _heads(v.astype(f32))
    dshape = (2,) + r_h.shape
    r2 = jnp.broadcast_to(r_h, dshape)
    v2 = jnp.broadcast_to(v_h, dshape)
    aa = jnp.broadcast_to(-kk, dshape)
    bb = kk[None] * a_h
    seq = tuple(jnp.moveaxis(_flip_bwd(z), 2, 0) for z in (r2, w_h, k_h, v2, aa, bb))
    S0 = jnp.zeros((2, b, N_HEADS, HEAD_DIM, HEAD_DIM), f32)
    _, ys = lax.scan(_wkv7_step, S0, seq)
    y = jnp.sum(_flip_bwd(jnp.moveaxis(ys, 0, 2)), axis=0)
    mean = jnp.mean(y, axis=-1, keepdims=True)
    var = jnp.mean(jnp.square(y - mean), axis=-1, keepdims=True)
    yn = ((y - mean) * lax.rsqrt(var + GN_EPS)).reshape(b, t, d) * ln_w.astype(f32) + ln_b.astype(f32)
    bonus = jnp.sum(jnp.sum(r_h[None] * k_h * r_k.astype(f32), axis=-1, keepdims=True), axis=0) * v_h
    out = ((yn + bonus.reshape(b, t, d)) * g.astype(f32)).astype(h.dtype) @ w_o
    return out, v_first


def conv_ffn(h, w_gate, w_up, conv_w, conv_b, w_down):
    gt = h @ w_gate
    gp = jnp.pad(gt, ((0, 0), (1, 1), (0, 0)))
    gc = gp[:, :-2] * conv_w[0] + gp[:, 1:-1] * conv_w[1] + gp[:, 2:] * conv_w[2] + conv_b
    return (jax.nn.silu(gc) * (h @ w_up)) @ w_down


def _trunk(x, norm_mix, norm_ffn, na_w_qkv, na_w_o, na_q_gain, na_k_gain, na_rpb,
           rw_mu, rw_w_r, rw_w_k, rw_w_v, rw_w_o, rw_w0, rw_w1, rw_w2, rw_a0, rw_a1, rw_a2,
           rw_v0, rw_v1, rw_v2, rw_g1, rw_g2, rw_k_k, rw_k_a, rw_r_k, rw_ln_w, rw_ln_b,
           ffn_w_gate, ffn_w_up, ffn_conv_w, ffn_conv_b, ffn_w_down):
    v_first = None
    for i in range(DEPTH):
        hn = _rms(x, norm_mix[i])
        li = i // N_MIXERS
        if i % N_MIXERS == 0:
            x = x + neighbourhood_attention(hn, na_w_qkv[li], na_w_o[li], na_q_gain[li], na_k_gain[li], na_rpb[li])
        else:
            vres = None if li == 0 else (rw_v0[li - 1], rw_v1[li - 1], rw_v2[li - 1])
            out, v_first = rwkv7_mixer(hn, v_first, vres, rw_mu[li], rw_w_r[li], rw_w_k[li], rw_w_v[li], rw_w_o[li],
                                       rw_w0[li], rw_w1[li], rw_w2[li], rw_a0[li], rw_a1[li], rw_a2[li],
                                       rw_g1[li], rw_g2[li], rw_k_k[li], rw_k_a[li], rw_r_k[li], rw_ln_w[li], rw_ln_b[li])
            x = x + out
        x = x + conv_ffn(_rms(x, norm_ffn[i]), ffn_w_gate[i], ffn_w_up[i], ffn_conv_w[i], ffn_conv_b[i], ffn_w_down[i])
    return x


def setup_inputs(seed: int = 0) -> dict:
    key = jax.random.key(seed)
    ks = iter(jax.random.split(key, 48))

    def nrm(shape, scale):
        return jax.random.normal(next(ks), shape, jnp.float32) * scale

    def unif(shape, lo, hi):
        return jax.random.uniform(next(ks), shape, jnp.float32, lo, hi)

    D, F, H, N = D_MODEL, D_FF, N_HEADS, HEAD_DIM
    nA, nB = N_A_LAYERS, N_B_LAYERS
    return {
        "x_prompt": nrm((BATCH, SEQ, D), 1.0),
        "x_sample": nrm((DEC_BATCH, DEC_SEQ, D), 1.0),
        "norm_mix": 1.0 + nrm((DEPTH, D), 0.05),
        "norm_ffn": 1.0 + nrm((DEPTH, D), 0.05),
        "na_w_qkv": nrm((nA, D, 3 * D), D ** -0.5),
        "na_w_o": nrm((nA, D, D), D ** -0.5),
        "na_q_gain": 1.0 + nrm((nA, N), 0.05),
        "na_k_gain": 1.0 + nrm((nA, N), 0.05),
        "na_rpb": nrm((nA, H, RPB_H, RPB_W), 0.5),
        "rw_mu": unif((nB, 6, D), 0.0, 1.0),
        "rw_w_r": nrm((nB, D, D), D ** -0.5),
        "rw_w_k": nrm((nB, D, D), D ** -0.5),
        "rw_w_v": nrm((nB, D, D), D ** -0.5),
        "rw_w_o": nrm((nB, D, D), D ** -0.5),
        "rw_w0": unif((nB, 2, D), -6.0, -1.0),
        "rw_w1": nrm((nB, 2, D, LORA_DECAY), D ** -0.5),
        "rw_w2": nrm((nB, 2, LORA_DECAY, D), 0.5 * LORA_DECAY ** -0.5),
        "rw_a0": nrm((nB, 2, D), 0.1),
        "rw_a1": nrm((nB, 2, D, LORA_ICLR), D ** -0.5),
        "rw_a2": nrm((nB, 2, LORA_ICLR, D), 0.5 * LORA_ICLR ** -0.5),
        "rw_v0": 1.0 + nrm((nB - 1, D), 0.1),
        "rw_v1": nrm((nB - 1, D, LORA_VALUE), D ** -0.5),
        "rw_v2": nrm((nB - 1, LORA_VALUE, D), 0.5 * LORA_VALUE ** -0.5),
        "rw_g1": nrm((nB, D, LORA_GATE), D ** -0.5),
        "rw_g2": nrm((nB, LORA_GATE, D), LORA_GATE ** -0.5),
        "rw_k_k": 0.85 + nrm((nB, D), 0.05),
        "rw_k_a": 1.0 + nrm((nB, D), 0.05),
        "rw_r_k": nrm((nB, H, N), 0.1),
        "rw_ln_w": 1.0 + nrm((nB, D), 0.05),
        "rw_ln_b": nrm((nB, D), 0.02),
        "ffn_w_gate": nrm((DEPTH, D, F), D ** -0.5),
        "ffn_w_up": nrm((DEPTH, D, F), D ** -0.5),
        "ffn_conv_w": nrm((DEPTH, CONV_W, F), CONV_W ** -0.5),
        "ffn_conv_b": nrm((DEPTH, F), 0.02),
        "ffn_w_down": nrm((DEPTH, F, D), F ** -0.5),
    }


def reference(x_prompt, x_sample, norm_mix, norm_ffn, na_w_qkv, na_w_o, na_q_gain, na_k_gain, na_rpb,
              rw_mu, rw_w_r, rw_w_k, rw_w_v, rw_w_o, rw_w0, rw_w1, rw_w2, rw_a0, rw_a1, rw_a2,
              rw_v0, rw_v1, rw_v2, rw_g1, rw_g2, rw_k_k, rw_k_a, rw_r_k, rw_ln_w, rw_ln_b,
              ffn_w_gate, ffn_w_up, ffn_conv_w, ffn_conv_b, ffn_w_down):
    weights = (norm_mix, norm_ffn, na_w_qkv, na_w_o, na_q_gain, na_k_gain, na_rpb,
               rw_mu, rw_w_r, rw_w_k, rw_w_v, rw_w_o, rw_w0, rw_w1, rw_w2, rw_a0, rw_a1, rw_a2,
               rw_v0, rw_v1, rw_v2, rw_g1, rw_g2, rw_k_k, rw_k_a, rw_r_k, rw_ln_w, rw_ln_b,
               ffn_w_gate, ffn_w_up, ffn_conv_w, ffn_conv_b, ffn_w_down)
    y_prompt = _trunk(x_prompt, *weights)
    y_sample = _trunk(x_sample, *weights)
    return (y_prompt, y_sample)
```

```python
import functools

import numpy as np
import jax
import jax.numpy as jnp
from jax import lax
from jax.experimental import pallas as pl
from jax.experimental.pallas import tpu as pltpu

F32 = jnp.float32
BF16 = jnp.bfloat16

D_MODEL = 1024
GRID_W = 64
N_HEADS = 16
HEAD_DIM = 64
WIN_H = 8
WIN_W = 16
RPB_H = 2 * WIN_H - 1
RPB_W = 2 * WIN_W - 1
D_FF = 2816
DEPTH = 4
RMS_EPS = 1e-6
GN_EPS = 64e-5
NEG_INF = -1e30

LANES = 128
HEADS_PER_TILE = LANES // HEAD_DIM
N_HEAD_TILES = D_MODEL // LANES
BF16_SUBLANES = 16
F32_SUBLANES = 8
CHUNK = 64
VMEM_LIMIT = 56 * 1024 * 1024


def _params(*sem):
    return pltpu.CompilerParams(dimension_semantics=sem, vmem_limit_bytes=VMEM_LIMIT)


def _const_spec(shape):
    nd = len(shape)
    return pl.BlockSpec(shape, lambda *_: (0,) * nd)


def _dot(a, b):
    return jnp.dot(a, b, preferred_element_type=F32)


def _dot_nt(a, b):
    return lax.dot_general(a, b, (((1,), (1,)), ((), ())), preferred_element_type=F32)


def _rms_rows(xf, g):
    ms = jnp.mean(xf * xf, axis=-1, keepdims=True)
    return xf * lax.rsqrt(ms + RMS_EPS) * g


def _head_sum(x_bf16, hsum):
    return _dot(x_bf16, hsum)


def _split2(x):
    hi = x.astype(BF16)
    lo = (x - hi.astype(F32)).astype(BF16)
    return hi, lo


def _split3(x):
    hi = x.astype(BF16)
    r1 = x - hi.astype(F32)
    mid = r1.astype(BF16)
    lo = (r1 - mid.astype(F32)).astype(BF16)
    return hi, mid, lo


def _qkv_kernel(x_ref, g_ref, w_ref, gq_ref, gk_ref, hs_ref, q_ref, k_ref, v_ref):
    hn = _rms_rows(x_ref[...], g_ref[...]).astype(BF16)
    hs = hs_ref[...]
    for part, (o_ref, gain_ref) in enumerate(((q_ref, gq_ref), (k_ref, gk_ref))):
        y = _dot(hn, w_ref[:, part * D_MODEL:(part + 1) * D_MODEL])
        gain = gain_ref[...]
        for t in range(N_HEAD_TILES):
            ys = y[:, t * LANES:(t + 1) * LANES]
            ms = _head_sum((ys * ys).astype(BF16), hs) * (1.0 / HEAD_DIM)
            o_ref[:, t * LANES:(t + 1) * LANES] = (ys * lax.rsqrt(ms + RMS_EPS) * gain).astype(BF16)
    v_ref[...] = _dot(hn, w_ref[:, 2 * D_MODEL:]).astype(BF16)


def _qkv_call(x2, g, w_qkv, gq, gk, hs, tm=512):
    n = x2.shape[0]
    tok = pl.BlockSpec((tm, D_MODEL), lambda i: (i, 0))
    out = jax.ShapeDtypeStruct((n, D_MODEL), BF16)
    return pl.pallas_call(
        _qkv_kernel,
        grid=(n // tm,),
        in_specs=[tok, _const_spec((1, D_MODEL)), _const_spec((D_MODEL, 3 * D_MODEL)),
                  _const_spec((1, LANES)), _const_spec((1, LANES)), _const_spec((LANES, LANES))],
        out_specs=[tok, tok, tok],
        out_shape=[out, out, out],
        compiler_params=_params("parallel"),
        name="na_qkv",
    )(x2, g, w_qkv, gq, gk, hs)


def _na_kernel(q_ref, k_ref, v_ref, tbl_ref, o_ref, *, rows):
    kh = min(WIN_H, rows)
    lane = lax.broadcasted_iota(jnp.int32, (GRID_W, LANES), 1)
    head_mask = [lane < HEAD_DIM, lane >= HEAD_DIM]

    def row_body(r, carry):
        rs = jnp.clip(r - kh // 2, 0, rows - kh)
        q = q_ref[0, pl.ds(pl.multiple_of(r * GRID_W, GRID_W), GRID_W), :]
        k = k_ref[0, pl.ds(pl.multiple_of(rs * GRID_W, GRID_W), kh * GRID_W), :]
        v = v_ref[0, pl.ds(pl.multiple_of(rs * GRID_W, GRID_W), kh * GRID_W), :]
        dr0 = rs - r + WIN_H - 1
        outs = []
        for hh in range(HEADS_PER_TILE):
            qm = jnp.where(head_mask[hh], q, jnp.zeros_like(q))
            s = _dot_nt(qm, k)
            bias = jnp.concatenate([tbl_ref[hh, dr0 + 2 * p] for p in range(kh // 2)], axis=1)
            s = s + bias
            m = jnp.max(s, axis=1, keepdims=True)
            p = jnp.exp(s - m)
            l = jnp.sum(p, axis=1, keepdims=True)
            outs.append(_dot(p.astype(BF16), v) / l)
        o = jnp.where(head_mask[0], outs[0], outs[1])
        o_ref[0, pl.ds(pl.multiple_of(r * GRID_W, GRID_W), GRID_W), :] = o.astype(BF16)
        return carry

    lax.fori_loop(0, rows, row_body, 0)


def _na_call(q, k, v, tbl):
    b, t, _ = q.shape
    rows = t // GRID_W
    assert rows >= WIN_H and WIN_H % 2 == 0
    blk = pl.BlockSpec((1, t, LANES), lambda bi, hi: (bi, 0, hi))
    return pl.pallas_call(
        functools.partial(_na_kernel, rows=rows),
        grid=(b, N_HEAD_TILES),
        in_specs=[blk, blk, blk,
                  pl.BlockSpec((HEADS_PER_TILE, RPB_H - 1, GRID_W, LANES), lambda bi, hi: (hi, 0, 0, 0))],
        out_specs=blk,
        out_shape=jax.ShapeDtypeStruct((b, t, D_MODEL), BF16),
        compiler_params=_params("parallel", "parallel"),
        name="na_attn",
    )(q, k, v, tbl)


def _na_bias_table(rpb):
    c = np.arange(GRID_W)[:, None]
    kc = np.arange(GRID_W)[None, :]
    cs = np.clip(c - WIN_W // 2, 0, GRID_W - WIN_W)
    ok = (kc >= cs) & (kc < cs + WIN_W)
    d_col = np.clip(kc - c + WIN_W - 1, 0, RPB_W - 1)
    base = jnp.where(ok[None, None], rpb.astype(F32)[:, :, d_col], NEG_INF)
    return jnp.concatenate([base[:, :-1], base[:, 1:]], axis=-1)


def _mm_res_kernel(a_ref, w_ref, x_ref, o_ref):
    o_ref[...] = x_ref[...] + _dot(a_ref[...], w_ref[...])


def _mm_res_call(a, w, x2, tm=512):
    n, kdim = a.shape
    tok = pl.BlockSpec((tm, D_MODEL), lambda i: (i, 0))
    return pl.pallas_call(
        _mm_res_kernel,
        grid=(n // tm,),
        in_specs=[pl.BlockSpec((tm, kdim), lambda i: (i, 0)), _const_spec((kdim, D_MODEL)), tok],
        out_specs=tok,
        out_shape=jax.ShapeDtypeStruct((n, D_MODEL), F32),
        compiler_params=_params("parallel"),
        name="mm_res",
    )(a, w, x2)


FF_HALF = D_FF // 2


def _ffn_in_kernel(x_ref, g_ref, wg_ref, wu_ref, gt_ref, up_ref):
    hn = _rms_rows(x_ref[...], g_ref[...]).astype(BF16)
    for w_ref, o_ref in ((wg_ref, gt_ref), (wu_ref, up_ref)):
        for h in range(2):
            sl = slice(h * FF_HALF, (h + 1) * FF_HALF)
            o_ref[:, sl] = _dot(hn, w_ref[:, sl]).astype(BF16)


def _ffn_in_call(x2, g, w_gate, w_up, tm=256):
    n = x2.shape[0]
    tok = pl.BlockSpec((tm, D_MODEL), lambda i: (i, 0))
    ff = pl.BlockSpec((tm, D_FF), lambda i: (i, 0))
    out = jax.ShapeDtypeStruct((n, D_FF), BF16)
    return pl.pallas_call(
        _ffn_in_kernel,
        grid=(n // tm,),
        in_specs=[tok, _const_spec((1, D_MODEL)), _const_spec((D_MODEL, D_FF)), _const_spec((D_MODEL, D_FF))],
        out_specs=[ff, ff],
        out_shape=[out, out],
        compiler_params=_params("parallel"),
        name="ffn_in",
    )(x2, g, w_gate, w_up)


def _shift_rows(cur, prev_row, next_row, tm):
    rid = lax.broadcasted_iota(jnp.int32, (tm, 1), 0)
    prev = jnp.where(rid == 0, prev_row, pltpu.roll(cur, 1, axis=0))
    nxt = jnp.where(rid == tm - 1, next_row, pltpu.roll(cur, tm - 1, axis=0))
    return prev, nxt


def _ffn_out_kernel(gt_ref, gp_ref, gn_ref, up_ref, cw_ref, cb_ref, wd_ref, x_ref, o_ref, *, tm, seq):
    t0 = pl.program_id(0) * tm
    first = (t0 % seq) == 0
    last = ((t0 + tm) % seq) == 0
    g = gt_ref[...].astype(F32)
    prev_row = jnp.where(first, 0.0, gp_ref[...].astype(F32)[BF16_SUBLANES - 1:BF16_SUBLANES, :])
    next_row = jnp.where(last, 0.0, gn_ref[...].astype(F32)[0:1, :])
    gprev, gnext = _shift_rows(g, prev_row, next_row, tm)
    cw = cw_ref[...]
    gc = gprev * cw[0:1, :] + g * cw[1:2, :] + gnext * cw[2:3, :] + cb_ref[...]
    act = (gc * jax.nn.sigmoid(gc) * up_ref[...].astype(F32)).astype(BF16)
    o_ref[...] = x_ref[...] + _dot(act, wd_ref[...])


def _ffn_out_call(gt, up, conv_w, conv_b, w_down, x2, seq, tm=256):
    n = x2.shape[0]
    hb = tm // BF16_SUBLANES
    nhb = n // BF16_SUBLANES
    tok = pl.BlockSpec((tm, D_MODEL), lambda i: (i, 0))
    ff = pl.BlockSpec((tm, D_FF), lambda i: (i, 0))
    return pl.pallas_call(
        functools.partial(_ffn_out_kernel, tm=tm, seq=seq),
        grid=(n // tm,),
        in_specs=[ff,
                  pl.BlockSpec((BF16_SUBLANES, D_FF), lambda i: (jnp.maximum(i * hb - 1, 0), 0)),
                  pl.BlockSpec((BF16_SUBLANES, D_FF), lambda i: (jnp.minimum((i + 1) * hb, nhb - 1), 0)),
                  ff, _const_spec((3, D_FF)), _const_spec((1, D_FF)), _const_spec((D_FF, D_MODEL)), tok],
        out_specs=tok,
        out_shape=jax.ShapeDtypeStruct((n, D_MODEL), F32),
        compiler_params=_params("parallel"),
        name="ffn_out",
    )(gt, gt, gt, up, conv_w, conv_b, w_down, x2)


def _rwkv_in_kernel(*refs, tm, seq, has_vres):
    (x_ref, xp_ref, xn_ref, g_ref, mu_ref, wr_ref, wk_ref, wv_ref, w1_ref, w2_ref, a1_ref, a2_ref,
     g1_ref, g2_ref, w0_ref, a0_ref, kk_ref, ka_ref, rk_ref, hs_ref) = refs[:20]
    refs = refs[20:]
    if has_vres:
        vf_ref, v0_ref, v1_ref, v2_ref = refs[:4]
        refs = refs[4:]
    r_out, v_out, aa_out, g_out, bonus_out, lw_out, kd_out, bb_out = refs

    t0 = pl.program_id(0) * tm
    first = (t0 % seq) == 0
    last = ((t0 + tm) % seq) == 0
    gn = g_ref[...]
    h = _rms_rows(x_ref[...], gn)
    prev_row = jnp.where(first, 0.0, _rms_rows(xp_ref[...], gn)[F32_SUBLANES - 1:F32_SUBLANES, :])
    next_row = jnp.where(last, 0.0, _rms_rows(xn_ref[...], gn)[0:1, :])
    prev, nxt = _shift_rows(h, prev_row, next_row, tm)
    xx = 0.5 * (prev + nxt) - h
    mu = mu_ref[...]
    xr, xw, xk, xv, xa, xg = ((h + xx * mu[i:i + 1, :]).astype(BF16) for i in range(6))

    r = _dot(xr, wr_ref[...])
    k = _dot(xk, wk_ref[...])
    v = _dot(xv, wv_ref[...])
    if has_vres:
        vl = _dot(_dot(xv, v1_ref[...]).astype(BF16), v2_ref[...])
        v = v + (vf_ref[...] - v) * jax.nn.sigmoid(v0_ref[...] + vl)
    g_out[...] = _dot(jax.nn.sigmoid(_dot(xg, g1_ref[...])).astype(BF16), g2_ref[...])
    lwl = _dot(jnp.tanh(_dot(xw, w1_ref[...])).astype(BF16), w2_ref[...])
    la = _dot(_dot(xa, a1_ref[...]).astype(BF16), a2_ref[...])

    hs = hs_ref[...]
    kk = k * kk_ref[...]
    kkn_parts = []
    for t in range(N_HEAD_TILES):
        ks = kk[:, t * LANES:(t + 1) * LANES]
        ss = _head_sum((ks * ks).astype(BF16), hs)
        kkn_parts.append(ks / jnp.maximum(jnp.sqrt(ss), 1e-12))
    kkn = jnp.concatenate(kkn_parts, axis=1)

    ka = ka_ref[...]
    kd_sum = None
    for z in range(2):
        zs = slice(z * D_MODEL, (z + 1) * D_MODEL)
        zin = w0_ref[:, zs] + lwl[:, zs]
        w_raw = jnp.minimum(zin, 0.0) - jnp.log(1.0 + jnp.exp(-jnp.abs(zin))) - 0.5
        lw_out[z] = -jnp.exp(w_raw)
        a = jax.nn.sigmoid(a0_ref[:, zs] + la[:, zs])
        kd = k * (1.0 + (a - 1.0) * ka)
        kd_out[z] = kd
        bb_out[z] = kkn * a
        kd_sum = kd if kd_sum is None else kd_sum + kd

    rkk = r * kd_sum * rk_ref[...]
    for t in range(N_HEAD_TILES):
        sl = slice(t * LANES, (t + 1) * LANES)
        hi, lo = _split2(rkk[:, sl])
        bonus_out[:, sl] = (_head_sum(hi, hs) + _head_sum(lo, hs)) * v[:, sl]
    r_out[...] = r
    v_out[...] = v
    aa_out[...] = -kkn


def _rwkv_in_call(x2, seq, p, v_first, tm=256):
    n = x2.shape[0]
    has_vres = v_first is not None
    hb = tm // F32_SUBLANES
    nhb = n // F32_SUBLANES
    tok = pl.BlockSpec((tm, D_MODEL), lambda i: (i, 0))
    tok2 = pl.BlockSpec((2, tm, D_MODEL), lambda i: (0, i, 0))
    halo_p = pl.BlockSpec((F32_SUBLANES, D_MODEL), lambda i: (jnp.maximum(i * hb - 1, 0), 0))
    halo_n = pl.BlockSpec((F32_SUBLANES, D_MODEL), lambda i: (jnp.minimum((i + 1) * hb, nhb - 1), 0))
    args = [x2, x2, x2, p["g"], p["mu"], p["w_r"], p["w_k"], p["w_v"], p["w1"], p["w2"], p["a1"], p["a2"],
            p["g1"], p["g2"], p["w0"], p["a0"], p["k_k"], p["k_a"], p["r_k"], p["hs"]]
    specs = [tok, halo_p, halo_n] + [_const_spec(a.shape) for a in args[3:]]
    if has_vres:
        extra = [v_first, p["v0"], p["v1"], p["v2"]]
        args += extra
        specs += [tok] + [_const_spec(a.shape) for a in extra[1:]]
    o1 = jax.ShapeDtypeStruct((n, D_MODEL), F32)
    o2 = jax.ShapeDtypeStruct((2, n, D_MODEL), F32)
    return pl.pallas_call(
        functools.partial(_rwkv_in_kernel, tm=tm, seq=seq, has_vres=has_vres),
        grid=(n // tm,),
        in_specs=specs,
        out_specs=[tok] * 5 + [tok2] * 3,
        out_shape=[o1] * 5 + [o2] * 3,
        compiler_params=_params("parallel"),
        name="rwkv_in",
    )(*args)


def _wkv_kernel(r_ref, v_ref, aa_ref, lw_ref, k_ref, bb_ref, y_ref, s_ref, *, tb):
    z = pl.program_id(0)
    j = pl.program_id(3)
    nchunk = tb // CHUNK
    sgn = 1 - 2 * z
    n2 = HEADS_PER_TILE * CHUNK

    @pl.when(j == 0)
    def _():
        s_ref[...] = jnp.zeros_like(s_ref)

    d = (lax.broadcasted_iota(jnp.int32, (n2, n2), 0) - lax.broadcasted_iota(jnp.int32, (n2, n2), 1)) * sgn
    strict = d > 0
    incl = d >= 0
    eye = (d == 0).astype(F32)
    dc = (lax.broadcasted_iota(jnp.int32, (CHUNK, CHUNK), 0)
          - lax.broadcasted_iota(jnp.int32, (CHUNK, CHUNK), 1)) * sgn
    tri = (dc >= 0).astype(BF16)
    lane = lax.broadcasted_iota(jnp.int32, (CHUNK, LANES), 1)
    m0 = lane < HEAD_DIM

    def stack(x):
        return jnp.concatenate([jnp.where(m0, x, 0.0), jnp.where(m0, 0.0, x)], axis=0).astype(BF16)

    def chunk_body(ci, carry):
        c = jnp.where(z == 0, ci, nchunk - 1 - ci)
        sl = pl.ds(pl.multiple_of(c * CHUNK, CHUNK), CHUNK)
        lw = lw_ref[0, 0, sl, :]
        h1, h2, h3 = _split3(lw)
        cl = _dot(tri, h1) + _dot(tri, h2) + _dot(tri, h3)
        e_in = jnp.exp(cl)
        e_ex = jnp.exp(cl - lw)
        e_neg = jnp.exp(-cl)
        wc = jnp.exp(jnp.sum(lw, axis=0, keepdims=True))
        rs = stack(r_ref[0, sl, :] * e_in)
        as_ = stack(aa_ref[0, sl, :] * e_ex)
        bs = stack(bb_ref[0, 0, sl, :] * e_neg)
        ks = stack(k_ref[0, 0, sl, :] * e_neg)
        vs_f = v_ref[0, sl, :]
        vs = stack(vs_f)
        a_ab = jnp.where(strict, _dot_nt(as_, bs), 0.0)
        a_ak = jnp.where(strict, _dot_nt(as_, ks), 0.0).astype(BF16)
        a_rb = jnp.where(incl, _dot_nt(rs, bs), 0.0).astype(BF16)
        a_rk = jnp.where(incl, _dot_nt(rs, ks), 0.0).astype(BF16)
        tm_ = eye + a_ab
        ak = a_ab.astype(BF16)
        for _ in range(CHUNK.bit_length() - 2):
            akf = _dot(ak, ak)
            ak = akf.astype(BF16)
            tm_ = tm_ + _dot(ak, tm_.astype(BF16))
        s = s_ref[...]
        s_b = s.astype(BF16)
        x = _dot_nt(as_, s_b) + _dot(a_ak, vs)
        u = _dot(tm_.astype(BF16), x.astype(BF16))
        u_b = u.astype(BF16)
        y = _dot_nt(rs, s_b) + _dot(a_rb, u_b) + _dot(a_rk, vs)
        y_ref[0, 0, sl, :] = y[:CHUNK] + y[CHUNK:]
        s_ref[...] = (s + _dot(u.T.astype(BF16), bs) + _dot(vs.astype(F32).T.astype(BF16), ks)) * wc
        return carry

    lax.fori_loop(0, nchunk, chunk_body, 0)


def _wkv_call(r, v, aa, lw, kd, bb, tb=512):
    b, t, _ = r.shape
    tb = min(tb, t)
    ntb = t // tb

    def tix(z, j):
        return jnp.where(z == 0, j, ntb - 1 - j)

    shared = pl.BlockSpec((1, tb, LANES), lambda z, bi, hi, j: (bi, tix(z, j), hi))
    perdir = pl.BlockSpec((1, 1, tb, LANES), lambda z, bi, hi, j: (z, bi, tix(z, j), hi))
    return pl.pallas_call(
        functools.partial(_wkv_kernel, tb=tb),
        grid=(2, b, N_HEAD_TILES, ntb),
        in_specs=[shared, shared, shared, perdir, perdir, perdir],
        out_specs=perdir,
        out_shape=jax.ShapeDtypeStruct((2, b, t, D_MODEL), F32),
        scratch_shapes=[pltpu.VMEM((LANES, LANES), F32)],
        compiler_params=_params("parallel", "parallel", "parallel", "arbitrary"),
        name="wkv",
    )(r, v, aa, lw, kd, bb)


def _rwkv_out_kernel(y_ref, bonus_ref, g_ref, lnw_ref, lnb_ref, hs_ref, wo_ref, x_ref, o_ref):
    hs = hs_ref[...]
    y = y_ref[0] + y_ref[1]
    parts = []
    for t in range(N_HEAD_TILES):
        ys = y[:, t * LANES:(t + 1) * LANES]
        hi, lo = _split2(ys)
        mean = (_head_sum(hi, hs) + _head_sum(lo, hs)) * (1.0 / HEAD_DIM)
        dlt = ys - mean
        var = _head_sum((dlt * dlt).astype(BF16), hs) * (1.0 / HEAD_DIM)
        parts.append(dlt * lax.rsqrt(var + GN_EPS))
    yn = jnp.concatenate(parts, axis=1) * lnw_ref[...] + lnb_ref[...]
    zz = ((yn + bonus_ref[...]) * g_ref[...]).astype(BF16)
    o_ref[...] = x_ref[...] + _dot(zz, wo_ref[...])


def _rwkv_out_call(y2, bonus, g, ln_w, ln_b, hs, w_o, x2, tm=256):
    n = x2.shape[0]
    tok = pl.BlockSpec((tm, D_MODEL), lambda i: (i, 0))
    return pl.pallas_call(
        _rwkv_out_kernel,
        grid=(n // tm,),
        in_specs=[pl.BlockSpec((2, tm, D_MODEL), lambda i: (0, i, 0)), tok, tok,
                  _const_spec((1, D_MODEL)), _const_spec((1, D_MODEL)), _const_spec((LANES, LANES)),
                  _const_spec((D_MODEL, D_MODEL)), tok],
        out_specs=tok,
        out_shape=jax.ShapeDtypeStruct((n, D_MODEL), F32),
        compiler_params=_params("parallel"),
        name="rwkv_out",
    )(y2, bonus, g, ln_w, ln_b, hs, w_o, x2)


def _row(v):
    return v.reshape(1, -1).astype(F32)


def _pad_to(a, axis, size):
    pad = [(0, 0)] * a.ndim
    pad[axis] = (0, size - a.shape[axis])
    return jnp.pad(a, pad)


def _block_diag2(m):
    l, d = m.shape[1], m.shape[2]
    zero = jnp.zeros((l, d), m.dtype)
    return jnp.concatenate([jnp.concatenate([m[0], zero], axis=1), jnp.concatenate([zero, m[1]], axis=1)], axis=0)


def _round_up(n, m):
    return (n + m - 1) // m * m


def _prepare(w):
    hs = np.kron(np.eye(HEADS_PER_TILE), np.ones((HEAD_DIM, HEAD_DIM)))
    hs = jnp.asarray(hs, BF16)
    na = []
    for li in range(w["na_w_qkv"].shape[0]):
        na.append(dict(
            w_qkv=w["na_w_qkv"][li].astype(BF16),
            w_o=w["na_w_o"][li].astype(BF16),
            gq=_row(jnp.tile(w["na_q_gain"][li], HEADS_PER_TILE)) * (HEAD_DIM ** -0.5),
            gk=_row(jnp.tile(w["na_k_gain"][li], HEADS_PER_TILE)),
            tbl=_na_bias_table(w["na_rpb"][li]),
        ))
    rw = []
    for li in range(w["rw_w_r"].shape[0]):
        lg = _round_up(w["rw_g1"].shape[-1], LANES)
        p = dict(
            mu=w["rw_mu"][li].astype(F32),
            w_r=w["rw_w_r"][li].astype(BF16), w_k=w["rw_w_k"][li].astype(BF16),
            w_v=w["rw_w_v"][li].astype(BF16), w_o=w["rw_w_o"][li].astype(BF16),
            w1=jnp.concatenate([w["rw_w1"][li, 0], w["rw_w1"][li, 1]], axis=1).astype(BF16),
            w2=_block_diag2(w["rw_w2"][li]).astype(BF16),
            a1=jnp.concatenate([w["rw_a1"][li, 0], w["rw_a1"][li, 1]], axis=1).astype(BF16),
            a2=_block_diag2(w["rw_a2"][li]).astype(BF16),
            g1=_pad_to(w["rw_g1"][li], 1, lg).astype(BF16),
            g2=_pad_to(w["rw_g2"][li], 0, lg).astype(BF16),
            w0=_row(w["rw_w0"][li]), a0=_row(w["rw_a0"][li]),
            k_k=_row(w["rw_k_k"][li]), k_a=_row(w["rw_k_a"][li]), r_k=_row(w["rw_r_k"][li]),
            ln_w=_row(w["rw_ln_w"][li]), ln_b=_row(w["rw_ln_b"][li]),
            hs=hs,
        )
        if li >= 1:
            lv = _round_up(w["rw_v1"].shape[-1], LANES)
            p.update(v0=_row(w["rw_v0"][li - 1]),
                     v1=_pad_to(w["rw_v1"][li - 1], 1, lv).astype(BF16),
                     v2=_pad_to(w["rw_v2"][li - 1], 0, lv).astype(BF16))
        rw.append(p)
    ffn = []
    for i in range(DEPTH):
        ffn.append(dict(
            w_gate=w["ffn_w_gate"][i].astype(BF16), w_up=w["ffn_w_up"][i].astype(BF16),
            conv_w=w["ffn_conv_w"][i].astype(F32), conv_b=_row(w["ffn_conv_b"][i]),
            w_down=w["ffn_w_down"][i].astype(BF16),
        ))
    return dict(na=na, rw=rw, ffn=ffn, hs=hs,
                norm_mix=w["norm_mix"].astype(F32), norm_ffn=w["norm_ffn"].astype(F32))


def _trunk(x, p):
    b, t, d = x.shape
    n = b * t
    x2 = x.reshape(n, d)
    v_first = None
    for i in range(DEPTH):
        li = i // 2
        g_mix = _row(p["norm_mix"][i])
        if i % 2 == 0:
            a = p["na"][li]
            q, k, v = _qkv_call(x2, g_mix, a["w_qkv"], a["gq"], a["gk"], p["hs"])
            o = _na_call(q.reshape(b, t, d), k.reshape(b, t, d), v.reshape(b, t, d), a["tbl"])
            x2 = _mm_res_call(o.reshape(n, d), a["w_o"], x2)
        else:
            rp = dict(p["rw"][li], g=g_mix)
            r, v, aa, g, bonus, lw, kd, bb = _rwkv_in_call(x2, t, rp, v_first)
            if v_first is None:
                v_first = v
            y2 = _wkv_call(r.reshape(b, t, d), v.reshape(b, t, d), aa.reshape(b, t, d),
                           lw.reshape(2, b, t, d), kd.reshape(2, b, t, d), bb.reshape(2, b, t, d))
            x2 = _rwkv_out_call(y2.reshape(2, n, d), bonus, g, rp["ln_w"], rp["ln_b"], p["hs"], rp["w_o"], x2)
        f = p["ffn"][i]
        gt, up = _ffn_in_call(x2, _row(p["norm_ffn"][i]), f["w_gate"], f["w_up"])
        x2 = _ffn_out_call(gt, up, f["conv_w"], f["conv_b"], f["w_down"], x2, t)
    return x2.reshape(b, t, d)


def kernel(x_prompt, x_sample, norm_mix, norm_ffn, na_w_qkv, na_w_o, na_q_gain, na_k_gain, na_rpb, rw_mu, rw_w_r, rw_w_k, rw_w_v, rw_w_o, rw_w0, rw_w1, rw_w2, rw_a0, rw_a1, rw_a2, rw_v0, rw_v1, rw_v2, rw_g1, rw_g2, rw_k_k, rw_k_a, rw_r_k, rw_ln_w, rw_ln_b, ffn_w_gate, ffn_w_up, ffn_conv_w, ffn_conv_b, ffn_w_down):
    w = dict(norm_mix=norm_mix, norm_ffn=norm_ffn, na_w_qkv=na_w_qkv, na_w_o=na_w_o, na_q_gain=na_q_gain,
             na_k_gain=na_k_gain, na_rpb=na_rpb, rw_mu=rw_mu, rw_w_r=rw_w_r, rw_w_k=rw_w_k, rw_w_v=rw_w_v,
             rw_w_o=rw_w_o, rw_w0=rw_w0, rw_w1=rw_w1, rw_w2=rw_w2, rw_a0=rw_a0, rw_a1=rw_a1, rw_a2=rw_a2,
             rw_v0=rw_v0, rw_v1=rw_v1, rw_v2=rw_v2, rw_g1=rw_g1, rw_g2=rw_g2, rw_k_k=rw_k_k, rw_k_a=rw_k_a,
             rw_r_k=rw_r_k, rw_ln_w=rw_ln_w, rw_ln_b=rw_ln_b, ffn_w_gate=ffn_w_gate, ffn_w_up=ffn_w_up,
             ffn_conv_w=ffn_conv_w, ffn_conv_b=ffn_conv_b, ffn_w_down=ffn_w_down)
    p = _prepare(w)
    return (_trunk(x_prompt, p), _trunk(x_sample, p))
```

```python
import functools

import numpy as np
import jax
import jax.numpy as jnp
from jax import lax
from jax.experimental import pallas as pl
from jax.experimental.pallas import tpu as pltpu

F32 = jnp.float32
BF16 = jnp.bfloat16

D_MODEL = 1024
GRID_W = 64
N_HEADS = 16
HEAD_DIM = 64
WIN_H = 8
WIN_W = 16
RPB_H = 2 * WIN_H - 1
RPB_W = 2 * WIN_W - 1
D_FF = 2816
DEPTH = 4
RMS_EPS = 1e-6
GN_EPS = 64e-5
NEG_INF = -1e30

LANES = 128
HEADS_PER_TILE = LANES // HEAD_DIM
N_HEAD_TILES = D_MODEL // LANES
BF16_SUBLANES = 16
F32_SUBLANES = 8
CHUNK = 64
NA_ROWS_PER_STEP = 4
VMEM_LIMIT = 56 * 1024 * 1024


def _params(*sem):
    return pltpu.CompilerParams(dimension_semantics=sem, vmem_limit_bytes=VMEM_LIMIT)


def _const_spec(shape):
    nd = len(shape)
    return pl.BlockSpec(shape, lambda *_: (0,) * nd)


def _dot(a, b):
    return jnp.dot(a, b, preferred_element_type=F32)


def _dot_nt(a, b):
    return lax.dot_general(a, b, (((1,), (1,)), ((), ())), preferred_element_type=F32)


def _rms_rows(xf, g):
    ms = jnp.mean(xf * xf, axis=-1, keepdims=True)
    return xf * lax.rsqrt(ms + RMS_EPS) * g


def _head_sum(x_bf16, hsum):
    return _dot(x_bf16, hsum)


def _split2(x):
    hi = x.astype(BF16)
    lo = (x - hi.astype(F32)).astype(BF16)
    return hi, lo


def _split3(x):
    hi = x.astype(BF16)
    r1 = x - hi.astype(F32)
    mid = r1.astype(BF16)
    lo = (r1 - mid.astype(F32)).astype(BF16)
    return hi, mid, lo


def _qkv_kernel(x_ref, g_ref, w_ref, gq_ref, gk_ref, hs_ref, q_ref, k_ref, v_ref):
    hn = _rms_rows(x_ref[...], g_ref[...]).astype(BF16)
    hs = hs_ref[...]
    for part, (o_ref, gain_ref) in enumerate(((q_ref, gq_ref), (k_ref, gk_ref))):
        y = _dot(hn, w_ref[:, part * D_MODEL:(part + 1) * D_MODEL])
        gain = gain_ref[...]
        for t in range(N_HEAD_TILES):
            ys = y[:, t * LANES:(t + 1) * LANES]
            ms = _head_sum((ys * ys).astype(BF16), hs) * (1.0 / HEAD_DIM)
            o_ref[:, t * LANES:(t + 1) * LANES] = (ys * lax.rsqrt(ms + RMS_EPS) * gain).astype(BF16)
    v_ref[...] = _dot(hn, w_ref[:, 2 * D_MODEL:]).astype(BF16)


def _qkv_call(x2, g, w_qkv, gq, gk, hs, tm=512):
    n = x2.shape[0]
    tok = pl.BlockSpec((tm, D_MODEL), lambda i: (i, 0))
    out = jax.ShapeDtypeStruct((n, D_MODEL), BF16)
    return pl.pallas_call(
        _qkv_kernel,
        grid=(n // tm,),
        in_specs=[tok, _const_spec((1, D_MODEL)), _const_spec((D_MODEL, 3 * D_MODEL)),
                  _const_spec((1, LANES)), _const_spec((1, LANES)), _const_spec((LANES, LANES))],
        out_specs=[tok, tok, tok],
        out_shape=[out, out, out],
        compiler_params=_params("parallel"),
        name="na_qkv",
    )(x2, g, w_qkv, gq, gk, hs)


def _na_kernel(q_ref, k_ref, v_ref, tbl_ref, o_ref, *, rows):
    kh = min(WIN_H, rows)
    lane = lax.broadcasted_iota(jnp.int32, (GRID_W, LANES), 1)
    head_mask = [lane < HEAD_DIM, lane >= HEAD_DIM]

    def rows_body(rb, carry):
        st = []
        for ri in range(NA_ROWS_PER_STEP):
            r = rb * NA_ROWS_PER_STEP + ri
            rs = jnp.clip(r - kh // 2, 0, rows - kh)
            qsl = pl.ds(pl.multiple_of(r * GRID_W, GRID_W), GRID_W)
            ksl = pl.ds(pl.multiple_of(rs * GRID_W, GRID_W), kh * GRID_W)
            q = q_ref[0, qsl, :]
            k = k_ref[0, ksl, :]
            dr0 = rs - r + WIN_H - 1
            for hh in range(HEADS_PER_TILE):
                qm = jnp.where(head_mask[hh], q, jnp.zeros_like(q))
                st.append(dict(hh=hh, dr0=dr0, ksl=ksl, qsl=qsl, s=_dot_nt(qm, k)))
        for e in st:
            bias = jnp.concatenate([tbl_ref[e["hh"], e["dr0"] + 2 * p] for p in range(kh // 2)], axis=1)
            s = e.pop("s") + bias
            m = jnp.max(s, axis=1, keepdims=True)
            p = jnp.exp(s - m)
            e["l"] = jnp.sum(p, axis=1, keepdims=True)
            e["p"] = p.astype(BF16)
        for e in st:
            e["o"] = _dot(e.pop("p"), v_ref[0, e["ksl"], :]) / e["l"]
        for e0, e1 in zip(st[0::2], st[1::2]):
            o_ref[0, e0["qsl"], :] = jnp.where(head_mask[0], e0["o"], e1["o"]).astype(BF16)
        return carry

    lax.fori_loop(0, rows // NA_ROWS_PER_STEP, rows_body, 0)


def _na_call(q, k, v, tbl):
    b, t, _ = q.shape
    rows = t // GRID_W
    assert rows >= WIN_H and WIN_H % 2 == 0
    blk = pl.BlockSpec((1, t, LANES), lambda bi, hi: (bi, 0, hi))
    return pl.pallas_call(
        functools.partial(_na_kernel, rows=rows),
        grid=(b, N_HEAD_TILES),
        in_specs=[blk, blk, blk,
                  pl.BlockSpec((HEADS_PER_TILE, RPB_H - 1, GRID_W, LANES), lambda bi, hi: (hi, 0, 0, 0))],
        out_specs=blk,
        out_shape=jax.ShapeDtypeStruct((b, t, D_MODEL), BF16),
        compiler_params=_params("parallel", "parallel"),
        name="na_attn",
    )(q, k, v, tbl)


def _na_bias_table(rpb):
    c = np.arange(GRID_W)[:, None]
    kc = np.arange(GRID_W)[None, :]
    cs = np.clip(c - WIN_W // 2, 0, GRID_W - WIN_W)
    ok = (kc >= cs) & (kc < cs + WIN_W)
    d_col = np.clip(kc - c + WIN_W - 1, 0, RPB_W - 1)
    base = jnp.where(ok[None, None], rpb.astype(F32)[:, :, d_col], NEG_INF)
    return jnp.concatenate([base[:, :-1], base[:, 1:]], axis=-1)


def _mm_res_kernel(a_ref, w_ref, x_ref, o_ref):
    o_ref[...] = x_ref[...] + _dot(a_ref[...], w_ref[...])


def _mm_res_call(a, w, x2, tm=512):
    n, kdim = a.shape
    tok = pl.BlockSpec((tm, D_MODEL), lambda i: (i, 0))
    return pl.pallas_call(
        _mm_res_kernel,
        grid=(n // tm,),
        in_specs=[pl.BlockSpec((tm, kdim), lambda i: (i, 0)), _const_spec((kdim, D_MODEL)), tok],
        out_specs=tok,
        out_shape=jax.ShapeDtypeStruct((n, D_MODEL), F32),
        compiler_params=_params("parallel"),
        name="mm_res",
    )(a, w, x2)


FF_HALF = D_FF // 2


def _ffn_in_kernel(x_ref, g_ref, wg_ref, wu_ref, gt_ref, up_ref):
    hn = _rms_rows(x_ref[...], g_ref[...]).astype(BF16)
    for w_ref, o_ref in ((wg_ref, gt_ref), (wu_ref, up_ref)):
        for h in range(2):
            sl = slice(h * FF_HALF, (h + 1) * FF_HALF)
            o_ref[:, sl] = _dot(hn, w_ref[:, sl]).astype(BF16)


def _ffn_in_call(x2, g, w_gate, w_up, tm=256):
    n = x2.shape[0]
    tok = pl.BlockSpec((tm, D_MODEL), lambda i: (i, 0))
    ff = pl.BlockSpec((tm, D_FF), lambda i: (i, 0))
    out = jax.ShapeDtypeStruct((n, D_FF), BF16)
    return pl.pallas_call(
        _ffn_in_kernel,
        grid=(n // tm,),
        in_specs=[tok, _const_spec((1, D_MODEL)), _const_spec((D_MODEL, D_FF)), _const_spec((D_MODEL, D_FF))],
        out_specs=[ff, ff],
        out_shape=[out, out],
        compiler_params=_params("parallel"),
        name="ffn_in",
    )(x2, g, w_gate, w_up)


def _shift_rows(cur, prev_row, next_row, tm):
    rid = lax.broadcasted_iota(jnp.int32, (tm, 1), 0)
    prev = jnp.where(rid == 0, prev_row, pltpu.roll(cur, 1, axis=0))
    nxt = jnp.where(rid == tm - 1, next_row, pltpu.roll(cur, tm - 1, axis=0))
    return prev, nxt


def _ffn_out_kernel(gt_ref, gp_ref, gn_ref, up_ref, cw_ref, cb_ref, wd_ref, x_ref, o_ref, *, tm, seq):
    t0 = pl.program_id(0) * tm
    first = (t0 % seq) == 0
    last = ((t0 + tm) % seq) == 0
    g = gt_ref[...].astype(F32)
    prev_row = jnp.where(first, 0.0, gp_ref[...].astype(F32)[BF16_SUBLANES - 1:BF16_SUBLANES, :])
    next_row = jnp.where(last, 0.0, gn_ref[...].astype(F32)[0:1, :])
    gprev, gnext = _shift_rows(g, prev_row, next_row, tm)
    cw = cw_ref[...]
    gc = gprev * cw[0:1, :] + g * cw[1:2, :] + gnext * cw[2:3, :] + cb_ref[...]
    act = (gc * jax.nn.sigmoid(gc) * up_ref[...].astype(F32)).astype(BF16)
    o_ref[...] = x_ref[...] + _dot(act, wd_ref[...])


def _ffn_out_call(gt, up, conv_w, conv_b, w_down, x2, seq, tm=256):
    n = x2.shape[0]
    hb = tm // BF16_SUBLANES
    nhb = n // BF16_SUBLANES
    tok = pl.BlockSpec((tm, D_MODEL), lambda i: (i, 0))
    ff = pl.BlockSpec((tm, D_FF), lambda i: (i, 0))
    return pl.pallas_call(
        functools.partial(_ffn_out_kernel, tm=tm, seq=seq),
        grid=(n // tm,),
        in_specs=[ff,
                  pl.BlockSpec((BF16_SUBLANES, D_FF), lambda i: (jnp.maximum(i * hb - 1, 0), 0)),
                  pl.BlockSpec((BF16_SUBLANES, D_FF), lambda i: (jnp.minimum((i + 1) * hb, nhb - 1), 0)),
                  ff, _const_spec((3, D_FF)), _const_spec((1, D_FF)), _const_spec((D_FF, D_MODEL)), tok],
        out_specs=tok,
        out_shape=jax.ShapeDtypeStruct((n, D_MODEL), F32),
        compiler_params=_params("parallel"),
        name="ffn_out",
    )(gt, gt, gt, up, conv_w, conv_b, w_down, x2)


def _rwkv_in_kernel(*refs, tm, seq, has_vres):
    (x_ref, xp_ref, xn_ref, g_ref, mu_ref, wr_ref, wk_ref, wv_ref, w1_ref, w2_ref, a1_ref, a2_ref,
     g1_ref, g2_ref, w0_ref, a0_ref, kk_ref, ka_ref, rk_ref, hs_ref) = refs[:20]
    refs = refs[20:]
    if has_vres:
        vf_ref, v0_ref, v1_ref, v2_ref = refs[:4]
        refs = refs[4:]
    r_out, v_out, aa_out, g_out, bonus_out, lw_out, kd_out, bb_out = refs

    t0 = pl.program_id(0) * tm
    first = (t0 % seq) == 0
    last = ((t0 + tm) % seq) == 0
    gn = g_ref[...]
    h = _rms_rows(x_ref[...], gn)
    prev_row = jnp.where(first, 0.0, _rms_rows(xp_ref[...], gn)[F32_SUBLANES - 1:F32_SUBLANES, :])
    next_row = jnp.where(last, 0.0, _rms_rows(xn_ref[...], gn)[0:1, :])
    prev, nxt = _shift_rows(h, prev_row, next_row, tm)
    xx = 0.5 * (prev + nxt) - h
    mu = mu_ref[...]
    xr, xw, xk, xv, xa, xg = ((h + xx * mu[i:i + 1, :]).astype(BF16) for i in range(6))

    r = _dot(xr, wr_ref[...])
    k = _dot(xk, wk_ref[...])
    v = _dot(xv, wv_ref[...])
    if has_vres:
        vl = _dot(_dot(xv, v1_ref[...]).astype(BF16), v2_ref[...])
        v = v + (vf_ref[...] - v) * jax.nn.sigmoid(v0_ref[...] + vl)
    g_out[...] = _dot(jax.nn.sigmoid(_dot(xg, g1_ref[...])).astype(BF16), g2_ref[...])
    lwl = _dot(jnp.tanh(_dot(xw, w1_ref[...])).astype(BF16), w2_ref[...])
    la = _dot(_dot(xa, a1_ref[...]).astype(BF16), a2_ref[...])

    hs = hs_ref[...]
    kk = k * kk_ref[...]
    kkn_parts = []
    for t in range(N_HEAD_TILES):
        ks = kk[:, t * LANES:(t + 1) * LANES]
        ss = _head_sum((ks * ks).astype(BF16), hs)
        kkn_parts.append(ks / jnp.maximum(jnp.sqrt(ss), 1e-12))
    kkn = jnp.concatenate(kkn_parts, axis=1)

    ka = ka_ref[...]
    kd_sum = None
    for z in range(2):
        zs = slice(z * D_MODEL, (z + 1) * D_MODEL)
        zin = w0_ref[:, zs] + lwl[:, zs]
        w_raw = jnp.minimum(zin, 0.0) - jnp.log(1.0 + jnp.exp(-jnp.abs(zin))) - 0.5
        lw_out[z] = -jnp.exp(w_raw)
        a = jax.nn.sigmoid(a0_ref[:, zs] + la[:, zs])
        kd = k * (1.0 + (a - 1.0) * ka)
        kd_out[z] = kd
        bb_out[z] = kkn * a
        kd_sum = kd if kd_sum is None else kd_sum + kd

    rkk = r * kd_sum * rk_ref[...]
    for t in range(N_HEAD_TILES):
        sl = slice(t * LANES, (t + 1) * LANES)
        hi, lo = _split2(rkk[:, sl])
        bonus_out[:, sl] = (_head_sum(hi, hs) + _head_sum(lo, hs)) * v[:, sl]
    r_out[...] = r
    v_out[...] = v
    aa_out[...] = -kkn


def _rwkv_in_call(x2, seq, p, v_first, tm=256):
    n = x2.shape[0]
    has_vres = v_first is not None
    hb = tm // F32_SUBLANES
    nhb = n // F32_SUBLANES
    tok = pl.BlockSpec((tm, D_MODEL), lambda i: (i, 0))
    tok2 = pl.BlockSpec((2, tm, D_MODEL), lambda i: (0, i, 0))
    halo_p = pl.BlockSpec((F32_SUBLANES, D_MODEL), lambda i: (jnp.maximum(i * hb - 1, 0), 0))
    halo_n = pl.BlockSpec((F32_SUBLANES, D_MODEL), lambda i: (jnp.minimum((i + 1) * hb, nhb - 1), 0))
    args = [x2, x2, x2, p["g"], p["mu"], p["w_r"], p["w_k"], p["w_v"], p["w1"], p["w2"], p["a1"], p["a2"],
            p["g1"], p["g2"], p["w0"], p["a0"], p["k_k"], p["k_a"], p["r_k"], p["hs"]]
    specs = [tok, halo_p, halo_n] + [_const_spec(a.shape) for a in args[3:]]
    if has_vres:
        extra = [v_first, p["v0"], p["v1"], p["v2"]]
        args += extra
        specs += [tok] + [_const_spec(a.shape) for a in extra[1:]]
    o1 = jax.ShapeDtypeStruct((n, D_MODEL), F32)
    o2 = jax.ShapeDtypeStruct((2, n, D_MODEL), F32)
    return pl.pallas_call(
        functools.partial(_rwkv_in_kernel, tm=tm, seq=seq, has_vres=has_vres),
        grid=(n // tm,),
        in_specs=specs,
        out_specs=[tok] * 5 + [tok2] * 3,
        out_shape=[o1] * 5 + [o2] * 3,
        compiler_params=_params("parallel"),
        name="rwkv_in",
    )(*args)


def _wkv_kernel(rf_ref, vf_ref, af_ref, rb_ref, vb_ref, ab_ref, lwf_ref, kf_ref, bf_ref, lwb_ref, kb_ref, bb_ref,
                yf_ref, yb_ref, s_ref, *, tb, nt):
    nchunk = tb // CHUNK
    n2 = HEADS_PER_TILE * CHUNK

    @pl.when(pl.program_id(2) == 0)
    def _():
        s_ref[...] = jnp.zeros_like(s_ref)

    d = lax.broadcasted_iota(jnp.int32, (n2, n2), 0) - lax.broadcasted_iota(jnp.int32, (n2, n2), 1)
    dc = lax.broadcasted_iota(jnp.int32, (CHUNK, CHUNK), 0) - lax.broadcasted_iota(jnp.int32, (CHUNK, CHUNK), 1)
    eye = (d == 0).astype(F32)
    lane = lax.broadcasted_iota(jnp.int32, (CHUNK, LANES), 1)
    m0 = lane < HEAD_DIM
    dirs = (
        (rf_ref, vf_ref, af_ref, lwf_ref, kf_ref, bf_ref, yf_ref, d > 0, d >= 0, (dc >= 0).astype(BF16)),
        (rb_ref, vb_ref, ab_ref, lwb_ref, kb_ref, bb_ref, yb_ref, d < 0, d <= 0, (dc <= 0).astype(BF16)),
    )

    def stack(x):
        return jnp.concatenate([jnp.where(m0, x, 0.0), jnp.where(m0, 0.0, x)], axis=0).astype(BF16)

    def chunk_body(ci, carry):
        st = []
        for z, (r_ref, v_ref, a_ref, lw_ref, k_ref, b_ref, y_ref, strict, incl, tri) in enumerate(dirs):
            c = ci if z == 0 else nchunk - 1 - ci
            sl = pl.ds(pl.multiple_of(c * CHUNK, CHUNK), CHUNK)
            lw_all = lw_ref[0, 0, sl, :]
            h1, h2, h3 = _split3(lw_all)
            cl_all = _dot(tri, h1) + _dot(tri, h2) + _dot(tri, h3)
            for t in range(nt):
                ls = slice(t * LANES, (t + 1) * LANES)
                lw = lw_all[:, ls]
                cl = cl_all[:, ls]
                e_in = jnp.exp(cl)
                e_ex = jnp.exp(cl - lw)
                e_neg = jnp.exp(-cl)
                q = dict(z=z, t=t, sl=sl, ls=ls, y_ref=y_ref, strict=strict, incl=incl)
                q["wc"] = jnp.exp(jnp.sum(lw, axis=0, keepdims=True))
                q["rs"] = stack(r_ref[0, sl, ls] * e_in)
                q["as"] = stack(a_ref[0, sl, ls] * e_ex)
                q["bk"] = jnp.concatenate([stack(b_ref[0, 0, sl, ls] * e_neg), stack(k_ref[0, 0, sl, ls] * e_neg)],
                                          axis=0)
                q["vs"] = stack(v_ref[0, sl, ls])
                st.append(q)
        for q in st:
            q["g_a"] = _dot_nt(q["as"], q["bk"])
        for q in st:
            q["g_r"] = _dot_nt(q["rs"], q["bk"])
        for q in st:
            g_a, g_r = q.pop("g_a"), q.pop("g_r")
            a_ab = jnp.where(q["strict"], g_a[:, :n2], 0.0)
            q["a_ak"] = jnp.where(q["strict"], g_a[:, n2:], 0.0).astype(BF16)
            q["a_r"] = jnp.concatenate([jnp.where(q["incl"], g_r[:, :n2], 0.0),
                                        jnp.where(q["incl"], g_r[:, n2:], 0.0)], axis=1).astype(BF16)
            q["tm"] = eye + a_ab
            q["ak"] = a_ab.astype(BF16)
        for _ in range(CHUNK.bit_length() - 2):
            for q in st:
                q["ak"] = _dot(q["ak"], q["ak"]).astype(BF16)
            for q in st:
                q["tm"] = q["tm"] + _dot(q["ak"], q["tm"].astype(BF16))
        for q in st:
            q["s"] = s_ref[q["z"], q["t"]]
            q["s_b"] = q["s"].astype(BF16)
            q["x"] = _dot_nt(q["as"], q["s_b"]) + _dot(q["a_ak"], q["vs"])
        for q in st:
            q["u"] = _dot(q["tm"].astype(BF16), q["x"].astype(BF16))
        for q in st:
            uv = jnp.concatenate([q["u"].astype(BF16), q["vs"]], axis=0)
            y = _dot_nt(q["rs"], q["s_b"]) + _dot(q["a_r"], uv)
            q["y_ref"][0, q["sl"], q["ls"]] = y[:CHUNK] + y[CHUNK:]
        for q in st:
            uv_t = jnp.concatenate([q["u"].T, q["vs"].astype(F32).T], axis=1).astype(BF16)
            s_ref[q["z"], q["t"]] = (q["s"] + _dot(uv_t, q["bk"])) * q["wc"]
        return carry

    lax.fori_loop(0, nchunk, chunk_body, 0)


def _wkv_call(r, v, aa, lw, kd, bb, tb=256, nt=4):
    b, t, _ = r.shape
    tb = min(tb, t)
    ntb = t // tb
    w = nt * LANES
    fwd = pl.BlockSpec((1, tb, w), lambda bi, hi, j: (bi, j, hi))
    bwd = pl.BlockSpec((1, tb, w), lambda bi, hi, j: (bi, ntb - 1 - j, hi))
    fwd_d = pl.BlockSpec((1, 1, tb, w), lambda bi, hi, j: (0, bi, j, hi))
    bwd_d = pl.BlockSpec((1, 1, tb, w), lambda bi, hi, j: (1, bi, ntb - 1 - j, hi))
    out = jax.ShapeDtypeStruct((b, t, D_MODEL), F32)
    return pl.pallas_call(
        functools.partial(_wkv_kernel, tb=tb, nt=nt),
        grid=(b, N_HEAD_TILES // nt, ntb),
        in_specs=[fwd, fwd, fwd, bwd, bwd, bwd, fwd_d, fwd_d, fwd_d, bwd_d, bwd_d, bwd_d],
        out_specs=[fwd, bwd],
        out_shape=[out, out],
        scratch_shapes=[pltpu.VMEM((2, nt, LANES, LANES), F32)],
        compiler_params=_params("parallel", "parallel", "arbitrary"),
        name="wkv",
    )(r, v, aa, r, v, aa, lw, kd, bb, lw, kd, bb)


def _rwkv_out_kernel(yf_ref, yb_ref, bonus_ref, g_ref, lnw_ref, lnb_ref, hs_ref, wo_ref, x_ref, o_ref):
    hs = hs_ref[...]
    y = yf_ref[...] + yb_ref[...]
    parts = []
    for t in range(N_HEAD_TILES):
        ys = y[:, t * LANES:(t + 1) * LANES]
        hi, lo = _split2(ys)
        mean = (_head_sum(hi, hs) + _head_sum(lo, hs)) * (1.0 / HEAD_DIM)
        dlt = ys - mean
        var = _head_sum((dlt * dlt).astype(BF16), hs) * (1.0 / HEAD_DIM)
        parts.append(dlt * lax.rsqrt(var + GN_EPS))
    yn = jnp.concatenate(parts, axis=1) * lnw_ref[...] + lnb_ref[...]
    zz = ((yn + bonus_ref[...]) * g_ref[...]).astype(BF16)
    o_ref[...] = x_ref[...] + _dot(zz, wo_ref[...])


def _rwkv_out_call(yf, yb, bonus, g, ln_w, ln_b, hs, w_o, x2, tm=256):
    n = x2.shape[0]
    tok = pl.BlockSpec((tm, D_MODEL), lambda i: (i, 0))
    return pl.pallas_call(
        _rwkv_out_kernel,
        grid=(n // tm,),
        in_specs=[tok, tok, tok, tok,
                  _const_spec((1, D_MODEL)), _const_spec((1, D_MODEL)), _const_spec((LANES, LANES)),
                  _const_spec((D_MODEL, D_MODEL)), tok],
        out_specs=tok,
        out_shape=jax.ShapeDtypeStruct((n, D_MODEL), F32),
        compiler_params=_params("parallel"),
        name="rwkv_out",
    )(yf, yb, bonus, g, ln_w, ln_b, hs, w_o, x2)


def _row(v):
    return v.reshape(1, -1).astype(F32)


def _pad_to(a, axis, size):
    pad = [(0, 0)] * a.ndim
    pad[axis] = (0, size - a.shape[axis])
    return jnp.pad(a, pad)


def _block_diag2(m):
    l, d = m.shape[1], m.shape[2]
    zero = jnp.zeros((l, d), m.dtype)
    return jnp.concatenate([jnp.concatenate([m[0], zero], axis=1), jnp.concatenate([zero, m[1]], axis=1)], axis=0)


def _round_up(n, m):
    return (n + m - 1) // m * m


def _prepare(w):
    hs = np.kron(np.eye(HEADS_PER_TILE), np.ones((HEAD_DIM, HEAD_DIM)))
    hs = jnp.asarray(hs, BF16)
    na = []
    for li in range(w["na_w_qkv"].shape[0]):
        na.append(dict(
            w_qkv=w["na_w_qkv"][li].astype(BF16),
            w_o=w["na_w_o"][li].astype(BF16),
            gq=_row(jnp.tile(w["na_q_gain"][li], HEADS_PER_TILE)) * (HEAD_DIM ** -0.5),
            gk=_row(jnp.tile(w["na_k_gain"][li], HEADS_PER_TILE)),
            tbl=_na_bias_table(w["na_rpb"][li]),
        ))
    rw = []
    for li in range(w["rw_w_r"].shape[0]):
        lg = _round_up(w["rw_g1"].shape[-1], LANES)
        p = dict(
            mu=w["rw_mu"][li].astype(F32),
            w_r=w["rw_w_r"][li].astype(BF16), w_k=w["rw_w_k"][li].astype(BF16),
            w_v=w["rw_w_v"][li].astype(BF16), w_o=w["rw_w_o"][li].astype(BF16),
            w1=jnp.concatenate([w["rw_w1"][li, 0], w["rw_w1"][li, 1]], axis=1).astype(BF16),
            w2=_block_diag2(w["rw_w2"][li]).astype(BF16),
            a1=jnp.concatenate([w["rw_a1"][li, 0], w["rw_a1"][li, 1]], axis=1).astype(BF16),
            a2=_block_diag2(w["rw_a2"][li]).astype(BF16),
            g1=_pad_to(w["rw_g1"][li], 1, lg).astype(BF16),
            g2=_pad_to(w["rw_g2"][li], 0, lg).astype(BF16),
            w0=_row(w["rw_w0"][li]), a0=_row(w["rw_a0"][li]),
            k_k=_row(w["rw_k_k"][li]), k_a=_row(w["rw_k_a"][li]), r_k=_row(w["rw_r_k"][li]),
            ln_w=_row(w["rw_ln_w"][li]), ln_b=_row(w["rw_ln_b"][li]),
            hs=hs,
        )
        if li >= 1:
            lv = _round_up(w["rw_v1"].shape[-1], LANES)
            p.update(v0=_row(w["rw_v0"][li - 1]),
                     v1=_pad_to(w["rw_v1"][li - 1], 1, lv).astype(BF16),
                     v2=_pad_to(w["rw_v2"][li - 1], 0, lv).astype(BF16))
        rw.append(p)
    ffn = []
    for i in range(DEPTH):
        ffn.append(dict(
            w_gate=w["ffn_w_gate"][i].astype(BF16), w_up=w["ffn_w_up"][i].astype(BF16),
            conv_w=w["ffn_conv_w"][i].astype(F32), conv_b=_row(w["ffn_conv_b"][i]),
            w_down=w["ffn_w_down"][i].astype(BF16),
        ))
    return dict(na=na, rw=rw, ffn=ffn, hs=hs,
                norm_mix=w["norm_mix"].astype(F32), norm_ffn=w["norm_ffn"].astype(F32))


def _trunk(x, p):
    b, t, d = x.shape
    n = b * t
    x2 = x.reshape(n, d)
    v_first = None
    for i in range(DEPTH):
        li = i // 2
        g_mix = _row(p["norm_mix"][i])
        if i % 2 == 0:
            a = p["na"][li]
            q, k, v = _qkv_call(x2, g_mix, a["w_qkv"], a["gq"], a["gk"], p["hs"])
            o = _na_call(q.reshape(b, t, d), k.reshape(b, t, d), v.reshape(b, t, d), a["tbl"])
            x2 = _mm_res_call(o.reshape(n, d), a["w_o"], x2)
        else:
            rp = dict(p["rw"][li], g=g_mix)
            r, v, aa, g, bonus, lw, kd, bb = _rwkv_in_call(x2, t, rp, v_first)
            if v_first is None:
                v_first = v
            yf, yb = _wkv_call(r.reshape(b, t, d), v.reshape(b, t, d), aa.reshape(b, t, d),
                               lw.reshape(2, b, t, d), kd.reshape(2, b, t, d), bb.reshape(2, b, t, d))
            x2 = _rwkv_out_call(yf.reshape(n, d), yb.reshape(n, d), bonus, g, rp["ln_w"], rp["ln_b"], p["hs"],
                                rp["w_o"], x2)
        f = p["ffn"][i]
        gt, up = _ffn_in_call(x2, _row(p["norm_ffn"][i]), f["w_gate"], f["w_up"])
        x2 = _ffn_out_call(gt, up, f["conv_w"], f["conv_b"], f["w_down"], x2, t)
    return x2.reshape(b, t, d)


def kernel(x_prompt, x_sample, norm_mix, norm_ffn, na_w_qkv, na_w_o, na_q_gain, na_k_gain, na_rpb, rw_mu, rw_w_r, rw_w_k, rw_w_v, rw_w_o, rw_w0, rw_w1, rw_w2, rw_a0, rw_a1, rw_a2, rw_v0, rw_v1, rw_v2, rw_g1, rw_g2, rw_k_k, rw_k_a, rw_r_k, rw_ln_w, rw_ln_b, ffn_w_gate, ffn_w_up, ffn_conv_w, ffn_conv_b, ffn_w_down):
    w = dict(norm_mix=norm_mix, norm_ffn=norm_ffn, na_w_qkv=na_w_qkv, na_w_o=na_w_o, na_q_gain=na_q_gain,
             na_k_gain=na_k_gain, na_rpb=na_rpb, rw_mu=rw_mu, rw_w_r=rw_w_r, rw_w_k=rw_w_k, rw_w_v=rw_w_v,
             rw_w_o=rw_w_o, rw_w0=rw_w0, rw_w1=rw_w1, rw_w2=rw_w2, rw_a0=rw_a0, rw_a1=rw_a1, rw_a2=rw_a2,
             rw_v0=rw_v0, rw_v1=rw_v1, rw_v2=rw_v2, rw_g1=rw_g1, rw_g2=rw_g2, rw_k_k=rw_k_k, rw_k_a=rw_k_a,
             rw_r_k=rw_r_k, rw_ln_w=rw_ln_w, rw_ln_b=rw_ln_b, ffn_w_gate=ffn_w_gate, ffn_w_up=ffn_w_up,
             ffn_conv_w=ffn_conv_w, ffn_conv_b=ffn_conv_b, ffn_w_down=ffn_w_down)
    p = _prepare(w)
    return (_trunk(x_prompt, p), _trunk(x_sample, p))
```

```python
import functools
import math

import numpy as np
import jax
import jax.numpy as jnp
from jax import lax
from jax.experimental import pallas as pl
from jax.experimental.pallas import tpu as pltpu

F32 = jnp.float32
BF16 = jnp.bfloat16

D_MODEL = 1024
GRID_W = 64
N_HEADS = 16
HEAD_DIM = 64
WIN_H = 8
WIN_W = 16
RPB_H = 2 * WIN_H - 1
RPB_W = 2 * WIN_W - 1
D_FF = 2816
DEPTH = 4
RMS_EPS = 1e-6
GN_EPS = 64e-5
NEG_INF = -1e30

LANES = 128
HEADS_PER_TILE = LANES // HEAD_DIM
N_HEAD_TILES = D_MODEL // LANES
MXU_W = 256
N_MXU_TILES = D_MODEL // MXU_W
BF16_SUBLANES = 16
F32_SUBLANES = 8
CHUNK = 64
NA_ROWS_PER_STEP = 8
VMEM_LIMIT = 56 * 1024 * 1024


def _params(*sem):
    return pltpu.CompilerParams(dimension_semantics=sem, vmem_limit_bytes=VMEM_LIMIT)


def _const_spec(shape):
    nd = len(shape)
    return pl.BlockSpec(shape, lambda *_: (0,) * nd)


def _dot(a, b):
    return jnp.dot(a, b, preferred_element_type=F32)


def _dot_nt(a, b):
    return lax.dot_general(a, b, (((1,), (1,)), ((), ())), preferred_element_type=F32)


def _rms_rows(xf, g):
    ms = jnp.mean(xf * xf, axis=-1, keepdims=True)
    return xf * lax.rsqrt(ms + RMS_EPS) * g


def _head_sum(x_bf16, hsum):
    return _dot(x_bf16, hsum)


def _sigmoid(x):
    return 0.5 * jnp.tanh(0.5 * x) + 0.5


def _split2(x):
    hi = x.astype(BF16)
    lo = (x - hi.astype(F32)).astype(BF16)
    return hi, lo


def _split3(x):
    hi = x.astype(BF16)
    r1 = x - hi.astype(F32)
    mid = r1.astype(BF16)
    lo = (r1 - mid.astype(F32)).astype(BF16)
    return hi, mid, lo


def _qkv_kernel(x_ref, g_ref, w_ref, gq_ref, gk_ref, hs_ref, q_ref, k_ref, v_ref):
    hn = _rms_rows(x_ref[...], g_ref[...]).astype(BF16)
    hs = hs_ref[...]
    for part, (o_ref, gain_ref) in enumerate(((q_ref, gq_ref), (k_ref, gk_ref))):
        y = _dot(hn, w_ref[:, part * D_MODEL:(part + 1) * D_MODEL])
        gain = gain_ref[...]
        for t in range(N_MXU_TILES):
            sl = slice(t * MXU_W, (t + 1) * MXU_W)
            ys = y[:, sl]
            ms = _head_sum((ys * ys).astype(BF16), hs) * (1.0 / HEAD_DIM)
            o_ref[:, sl] = (ys * lax.rsqrt(ms + RMS_EPS) * gain[:, sl]).astype(BF16)
    v_ref[...] = _dot(hn, w_ref[:, 2 * D_MODEL:]).astype(BF16)


def _qkv_call(x2, g, w_qkv, gq, gk, hs, tm=512):
    n = x2.shape[0]
    tok = pl.BlockSpec((tm, D_MODEL), lambda i: (i, 0))
    out = jax.ShapeDtypeStruct((n, D_MODEL), BF16)
    return pl.pallas_call(
        _qkv_kernel,
        grid=(n // tm,),
        in_specs=[tok, _const_spec((1, D_MODEL)), _const_spec((D_MODEL, 3 * D_MODEL)),
                  _const_spec((1, D_MODEL)), _const_spec((1, D_MODEL)), _const_spec((MXU_W, MXU_W))],
        out_specs=[tok, tok, tok],
        out_shape=[out, out, out],
        compiler_params=_params("parallel"),
        name="na_qkv",
    )(x2, g, w_qkv, gq, gk, hs)


def _na_kernel(q_ref, k_ref, v_ref, tbl_ref, o_ref, *, rows):
    kh = min(WIN_H, rows)
    lane = lax.broadcasted_iota(jnp.int32, (GRID_W, LANES), 1)
    head_mask = [lane < HEAD_DIM, lane >= HEAD_DIM]

    def rows_body(rb, carry):
        st = []
        for ri in range(NA_ROWS_PER_STEP):
            r = rb * NA_ROWS_PER_STEP + ri
            rs = jnp.clip(r - kh // 2, 0, rows - kh)
            qsl = pl.ds(pl.multiple_of(r * GRID_W, GRID_W), GRID_W)
            ksl = pl.ds(pl.multiple_of(rs * GRID_W, GRID_W), kh * GRID_W)
            q = q_ref[0, qsl, :]
            zero = jnp.zeros_like(q)
            qs = jnp.concatenate([jnp.where(head_mask[0], q, zero), jnp.where(head_mask[1], q, zero)], axis=0)
            st.append(dict(dr0=rs - r + WIN_H - 1, ksl=ksl, qsl=qsl, s=_dot_nt(qs, k_ref[0, ksl, :])))
        for e in st:
            bias = jnp.concatenate([tbl_ref[0, e["dr0"] + 2 * p] for p in range(kh // 2)], axis=1)
            s = e.pop("s") + bias
            m = jnp.max(s, axis=1, keepdims=True)
            p = jnp.exp(s - m)
            e["l"] = jnp.sum(p, axis=1, keepdims=True)
            e["p"] = p.astype(BF16)
        for e in st:
            e["o"] = _dot(e.pop("p"), v_ref[0, e["ksl"], :]) / e["l"]
        for e in st:
            o = e["o"]
            o_ref[0, e["qsl"], :] = jnp.where(head_mask[0], o[:GRID_W], o[GRID_W:]).astype(BF16)
        return carry

    lax.fori_loop(0, rows // NA_ROWS_PER_STEP, rows_body, 0)


def _na_call(q, k, v, tbl):
    b, t, _ = q.shape
    rows = t // GRID_W
    assert rows >= WIN_H and WIN_H % 2 == 0
    blk = pl.BlockSpec((1, t, LANES), lambda bi, hi: (bi, 0, hi))
    return pl.pallas_call(
        functools.partial(_na_kernel, rows=rows),
        grid=(b, N_HEAD_TILES),
        in_specs=[blk, blk, blk,
                  pl.BlockSpec((1, RPB_H - 1, HEADS_PER_TILE * GRID_W, LANES), lambda bi, hi: (hi, 0, 0, 0))],
        out_specs=blk,
        out_shape=jax.ShapeDtypeStruct((b, t, D_MODEL), BF16),
        compiler_params=_params("parallel", "parallel"),
        name="na_attn",
    )(q, k, v, tbl)


def _na_bias_table(rpb):
    c = np.arange(GRID_W)[:, None]
    kc = np.arange(GRID_W)[None, :]
    cs = np.clip(c - WIN_W // 2, 0, GRID_W - WIN_W)
    ok = (kc >= cs) & (kc < cs + WIN_W)
    d_col = np.clip(kc - c + WIN_W - 1, 0, RPB_W - 1)
    base = jnp.where(ok[None, None], rpb.astype(F32)[:, :, d_col], NEG_INF)
    pairs = jnp.concatenate([base[:, :-1], base[:, 1:]], axis=-1)
    pairs = pairs.reshape(N_HEAD_TILES, HEADS_PER_TILE, RPB_H - 1, GRID_W, LANES)
    return jnp.transpose(pairs, (0, 2, 1, 3, 4)).reshape(N_HEAD_TILES, RPB_H - 1, HEADS_PER_TILE * GRID_W, LANES)


def _mm_res_kernel(a_ref, w_ref, x_ref, o_ref):
    o_ref[...] = x_ref[...] + _dot(a_ref[...], w_ref[...])


def _mm_res_call(a, w, x2, tm=512):
    n, kdim = a.shape
    tok = pl.BlockSpec((tm, D_MODEL), lambda i: (i, 0))
    return pl.pallas_call(
        _mm_res_kernel,
        grid=(n // tm,),
        in_specs=[pl.BlockSpec((tm, kdim), lambda i: (i, 0)), _const_spec((kdim, D_MODEL)), tok],
        out_specs=tok,
        out_shape=jax.ShapeDtypeStruct((n, D_MODEL), F32),
        compiler_params=_params("parallel"),
        name="mm_res",
    )(a, w, x2)


FF_HALF = D_FF // 2


def _ffn_in_kernel(x_ref, g_ref, wg_ref, wu_ref, gt_ref, up_ref):
    hn = _rms_rows(x_ref[...], g_ref[...]).astype(BF16)
    for w_ref, o_ref in ((wg_ref, gt_ref), (wu_ref, up_ref)):
        for h in range(2):
            sl = slice(h * FF_HALF, (h + 1) * FF_HALF)
            o_ref[:, sl] = _dot(hn, w_ref[:, sl]).astype(BF16)


def _ffn_in_call(x2, g, w_gate, w_up, tm=256):
    n = x2.shape[0]
    tok = pl.BlockSpec((tm, D_MODEL), lambda i: (i, 0))
    ff = pl.BlockSpec((tm, D_FF), lambda i: (i, 0))
    out = jax.ShapeDtypeStruct((n, D_FF), BF16)
    return pl.pallas_call(
        _ffn_in_kernel,
        grid=(n // tm,),
        in_specs=[tok, _const_spec((1, D_MODEL)), _const_spec((D_MODEL, D_FF)), _const_spec((D_MODEL, D_FF))],
        out_specs=[ff, ff],
        out_shape=[out, out],
        compiler_params=_params("parallel"),
        name="ffn_in",
    )(x2, g, w_gate, w_up)


def _shift_rows(cur, prev_row, next_row, tm):
    rid = lax.broadcasted_iota(jnp.int32, (F32_SUBLANES, 1), 0)
    down = pltpu.roll(cur, 1, axis=0)
    up = pltpu.roll(cur, tm - 1, axis=0)
    prev = jnp.concatenate([jnp.where(rid == 0, prev_row, down[:F32_SUBLANES]), down[F32_SUBLANES:]], axis=0)
    nxt = jnp.concatenate([up[:tm - F32_SUBLANES],
                           jnp.where(rid == F32_SUBLANES - 1, next_row, up[tm - F32_SUBLANES:])], axis=0)
    return prev, nxt


def _ffn_out_kernel(gt_ref, gp_ref, gn_ref, up_ref, cw_ref, cb_ref, wd_ref, x_ref, o_ref, *, tm, seq):
    t0 = pl.program_id(0) * tm
    first = (t0 % seq) == 0
    last = ((t0 + tm) % seq) == 0
    g = gt_ref[...].astype(F32)
    prev_row = jnp.where(first, 0.0, gp_ref[...].astype(F32)[BF16_SUBLANES - 1:BF16_SUBLANES, :])
    next_row = jnp.where(last, 0.0, gn_ref[...].astype(F32)[0:1, :])
    gprev, gnext = _shift_rows(g, prev_row, next_row, tm)
    cw = 0.5 * cw_ref[...]
    h = gprev * cw[0:1, :] + g * cw[1:2, :] + gnext * cw[2:3, :] + 0.5 * cb_ref[...]
    act = (h * (jnp.tanh(h) + 1.0)).astype(BF16) * up_ref[...]
    o_ref[...] = x_ref[...] + _dot(act, wd_ref[...])


def _ffn_out_call(gt, up, conv_w, conv_b, w_down, x2, seq, tm=256):
    n = x2.shape[0]
    hb = tm // BF16_SUBLANES
    nhb = n // BF16_SUBLANES
    tok = pl.BlockSpec((tm, D_MODEL), lambda i: (i, 0))
    ff = pl.BlockSpec((tm, D_FF), lambda i: (i, 0))
    return pl.pallas_call(
        functools.partial(_ffn_out_kernel, tm=tm, seq=seq),
        grid=(n // tm,),
        in_specs=[ff,
                  pl.BlockSpec((BF16_SUBLANES, D_FF), lambda i: (jnp.maximum(i * hb - 1, 0), 0)),
                  pl.BlockSpec((BF16_SUBLANES, D_FF), lambda i: (jnp.minimum((i + 1) * hb, nhb - 1), 0)),
                  ff, _const_spec((3, D_FF)), _const_spec((1, D_FF)), _const_spec((D_FF, D_MODEL)), tok],
        out_specs=tok,
        out_shape=jax.ShapeDtypeStruct((n, D_MODEL), F32),
        compiler_params=_params("parallel"),
        name="ffn_out",
    )(gt, gt, gt, up, conv_w, conv_b, w_down, x2)


def _rwkv_in_kernel(*refs, tm, seq, has_vres):
    (x_ref, xp_ref, xn_ref, g_ref, mu_ref, wr_ref, wk_ref, wv_ref, w1_ref, w2_ref, a1_ref, a2_ref,
     g1_ref, g2_ref, w0_ref, a0_ref, kk_ref, ka_ref, rk_ref, hs_ref) = refs[:20]
    refs = refs[20:]
    if has_vres:
        vf_ref, v0_ref, v1_ref, v2_ref = refs[:4]
        refs = refs[4:]
    r_out, v_out, aa_out, g_out, bonus_out, lw_out, kd_out, bb_out = refs

    t0 = pl.program_id(0) * tm
    first = (t0 % seq) == 0
    last = ((t0 + tm) % seq) == 0
    gn = g_ref[...]
    h = _rms_rows(x_ref[...], gn)
    prev_row = jnp.where(first, 0.0, _rms_rows(xp_ref[...], gn)[F32_SUBLANES - 1:F32_SUBLANES, :])
    next_row = jnp.where(last, 0.0, _rms_rows(xn_ref[...], gn)[0:1, :])
    prev, nxt = _shift_rows(h, prev_row, next_row, tm)
    xx = 0.5 * (prev + nxt) - h
    mu = mu_ref[...]
    xr, xw, xk, xv, xa, xg = ((h + xx * mu[i:i + 1, :]).astype(BF16) for i in range(6))

    r = _dot(xr, wr_ref[...])
    k = _dot(xk, wk_ref[...])
    v = _dot(xv, wv_ref[...])
    if has_vres:
        vl = _dot(_dot(xv, v1_ref[...]).astype(BF16), v2_ref[...])
        v = v + (vf_ref[...] - v) * _sigmoid(v0_ref[...] + vl)
    g_out[...] = _dot(_sigmoid(_dot(xg, g1_ref[...])).astype(BF16), g2_ref[...])
    lwl = _dot(jnp.tanh(_dot(xw, w1_ref[...])).astype(BF16), w2_ref[...])
    la = _dot(_dot(xa, a1_ref[...]).astype(BF16), a2_ref[...])

    hs = hs_ref[...]
    kk = k * kk_ref[...]
    kkn_parts = []
    for t in range(N_MXU_TILES):
        ks = kk[:, t * MXU_W:(t + 1) * MXU_W]
        ss = _head_sum((ks * ks).astype(BF16), hs)
        kkn_parts.append(ks * jnp.minimum(lax.rsqrt(ss), 1e12))
    kkn = jnp.concatenate(kkn_parts, axis=1)

    ka = ka_ref[...]
    kd_sum = None
    for z in range(2):
        zs = slice(z * D_MODEL, (z + 1) * D_MODEL)
        lw_out[z] = (-math.exp(-0.5)) * _sigmoid(w0_ref[:, zs] + lwl[:, zs])
        a = _sigmoid(a0_ref[:, zs] + la[:, zs])
        kd = k * (1.0 + (a - 1.0) * ka)
        kd_out[z] = kd
        bb_out[z] = kkn * a
        kd_sum = kd if kd_sum is None else kd_sum + kd

    rkk = r * kd_sum * rk_ref[...]
    for t in range(N_MXU_TILES):
        sl = slice(t * MXU_W, (t + 1) * MXU_W)
        hi, lo = _split2(rkk[:, sl])
        bonus_out[:, sl] = (_head_sum(hi, hs) + _head_sum(lo, hs)) * v[:, sl]
    r_out[...] = r
    v_out[...] = v
    aa_out[...] = -kkn


def _rwkv_in_call(x2, seq, p, v_first, tm=256):
    n = x2.shape[0]
    has_vres = v_first is not None
    hb = tm // F32_SUBLANES
    nhb = n // F32_SUBLANES
    tok = pl.BlockSpec((tm, D_MODEL), lambda i: (i, 0))
    tok2 = pl.BlockSpec((2, tm, D_MODEL), lambda i: (0, i, 0))
    halo_p = pl.BlockSpec((F32_SUBLANES, D_MODEL), lambda i: (jnp.maximum(i * hb - 1, 0), 0))
    halo_n = pl.BlockSpec((F32_SUBLANES, D_MODEL), lambda i: (jnp.minimum((i + 1) * hb, nhb - 1), 0))
    args = [x2, x2, x2, p["g"], p["mu"], p["w_r"], p["w_k"], p["w_v"], p["w1"], p["w2"], p["a1"], p["a2"],
            p["g1"], p["g2"], p["w0"], p["a0"], p["k_k"], p["k_a"], p["r_k"], p["hs"]]
    specs = [tok, halo_p, halo_n] + [_const_spec(a.shape) for a in args[3:]]
    if has_vres:
        extra = [v_first, p["v0"], p["v1"], p["v2"]]
        args += extra
        specs += [tok] + [_const_spec(a.shape) for a in extra[1:]]
    o1 = jax.ShapeDtypeStruct((n, D_MODEL), F32)
    o2 = jax.ShapeDtypeStruct((2, n, D_MODEL), F32)
    return pl.pallas_call(
        functools.partial(_rwkv_in_kernel, tm=tm, seq=seq, has_vres=has_vres),
        grid=(n // tm,),
        in_specs=specs,
        out_specs=[tok] * 5 + [tok2] * 3,
        out_shape=[o1] * 5 + [o2] * 3,
        compiler_params=_params("parallel"),
        name="rwkv_in",
    )(*args)


def _wkv_kernel(rf_ref, vf_ref, af_ref, rb_ref, vb_ref, ab_ref, lwf_ref, kf_ref, bf_ref, lwb_ref, kb_ref, bb_ref,
                yf_ref, yb_ref, s_ref, *, tb, nt):
    nchunk = tb // CHUNK
    n2 = HEADS_PER_TILE * CHUNK

    @pl.when(pl.program_id(2) == 0)
    def _():
        s_ref[...] = jnp.zeros_like(s_ref)

    d = lax.broadcasted_iota(jnp.int32, (n2, n2), 0) - lax.broadcasted_iota(jnp.int32, (n2, n2), 1)
    dc = lax.broadcasted_iota(jnp.int32, (CHUNK, CHUNK), 0) - lax.broadcasted_iota(jnp.int32, (CHUNK, CHUNK), 1)
    eye = (d == 0).astype(F32)
    lane = lax.broadcasted_iota(jnp.int32, (CHUNK, LANES), 1)
    m0 = lane < HEAD_DIM
    dirs = (
        (rf_ref, vf_ref, af_ref, lwf_ref, kf_ref, bf_ref, yf_ref, d > 0, d >= 0, (dc >= 0).astype(BF16)),
        (rb_ref, vb_ref, ab_ref, lwb_ref, kb_ref, bb_ref, yb_ref, d < 0, d <= 0, (dc <= 0).astype(BF16)),
    )

    def stack(x):
        return jnp.concatenate([jnp.where(m0, x, 0.0), jnp.where(m0, 0.0, x)], axis=0).astype(BF16)

    def chunk_body(ci, carry):
        st = []
        for z, (r_ref, v_ref, a_ref, lw_ref, k_ref, b_ref, y_ref, strict, incl, tri) in enumerate(dirs):
            c = ci if z == 0 else nchunk - 1 - ci
            sl = pl.ds(pl.multiple_of(c * CHUNK, CHUNK), CHUNK)
            lw_all = lw_ref[0, 0, sl, :]
            h1, h2, h3 = _split3(lw_all)
            cl_all = _dot(tri, h1) + _dot(tri, h2) + _dot(tri, h3)
            for t in range(nt):
                ls = slice(t * LANES, (t + 1) * LANES)
                lw = lw_all[:, ls]
                cl = cl_all[:, ls]
                e_in = jnp.exp(cl)
                e_ex = jnp.exp(cl - lw)
                e_neg = jnp.exp(-cl)
                q = dict(zi=z, t=t, sl=sl, ls=ls, y_ref=y_ref, strict=strict, incl=incl)
                q["wc"] = jnp.exp(jnp.sum(lw, axis=0, keepdims=True))
                q["rs"] = stack(r_ref[0, sl, ls] * e_in)
                q["as"] = stack(a_ref[0, sl, ls] * e_ex)
                q["bk"] = jnp.concatenate([stack(b_ref[0, 0, sl, ls] * e_neg), stack(k_ref[0, 0, sl, ls] * e_neg)],
                                          axis=0)
                q["vs"] = stack(v_ref[0, sl, ls])
                st.append(q)
        for q in st:
            q["ar"] = jnp.concatenate([q.pop("as"), q.pop("rs")], axis=0)
            q["g"] = _dot_nt(q["ar"], q["bk"])
        for q in st:
            g = q.pop("g")
            a_ab = jnp.where(q["strict"], g[:n2, :n2], 0.0)
            q["a_ak"] = jnp.where(q["strict"], g[:n2, n2:], 0.0).astype(BF16)
            q["a_r"] = jnp.concatenate([jnp.where(q["incl"], g[n2:, :n2], 0.0),
                                        jnp.where(q["incl"], g[n2:, n2:], 0.0)], axis=1).astype(BF16)
            q["tm"] = eye + a_ab
            q["pw"] = a_ab.astype(BF16)
        levels = CHUNK.bit_length() - 1
        for q in st:
            q["pw"] = _dot(q["pw"], q["pw"]).astype(BF16)
        for _ in range(levels - 2):
            for q in st:
                q["z"] = _dot(q["pw"], jnp.concatenate([q["pw"], q["tm"].astype(BF16)], axis=1))
            for q in st:
                z = q.pop("z")
                q["pw"] = z[:, :n2].astype(BF16)
                q["tm"] = q["tm"] + z[:, n2:]
        for q in st:
            q["tm"] = q["tm"] + _dot(q["pw"], q["tm"].astype(BF16))
        for q in st:
            q["s"] = s_ref[q["zi"], q["t"]]
            xy = _dot_nt(q["ar"], q["s"].astype(BF16))
            q["y_s"] = xy[n2:]
            q["x"] = xy[:n2] + _dot(q["a_ak"], q["vs"])
        for q in st:
            q["u"] = _dot(q["tm"].astype(BF16), q["x"].astype(BF16))
        for q in st:
            uv = jnp.concatenate([q["u"].astype(BF16), q["vs"]], axis=0)
            y = q["y_s"] + _dot(q["a_r"], uv)
            q["y_ref"][0, q["sl"], q["ls"]] = y[:CHUNK] + y[CHUNK:]
        for q in st:
            uv_t = jnp.concatenate([q["u"].T, q["vs"].astype(F32).T], axis=1).astype(BF16)
            s_ref[q["zi"], q["t"]] = (q["s"] + _dot(uv_t, q["bk"])) * q["wc"]
        return carry

    lax.fori_loop(0, nchunk, chunk_body, 0)


def _wkv_call(r, v, aa, lw, kd, bb, tb=256, nt=4):
    b, t, _ = r.shape
    tb = min(tb, t)
    ntb = t // tb
    w = nt * LANES
    fwd = pl.BlockSpec((1, tb, w), lambda bi, hi, j: (bi, j, hi))
    bwd = pl.BlockSpec((1, tb, w), lambda bi, hi, j: (bi, ntb - 1 - j, hi))
    fwd_d = pl.BlockSpec((1, 1, tb, w), lambda bi, hi, j: (0, bi, j, hi))
    bwd_d = pl.BlockSpec((1, 1, tb, w), lambda bi, hi, j: (1, bi, ntb - 1 - j, hi))
    out = jax.ShapeDtypeStruct((b, t, D_MODEL), F32)
    return pl.pallas_call(
        functools.partial(_wkv_kernel, tb=tb, nt=nt),
        grid=(b, N_HEAD_TILES // nt, ntb),
        in_specs=[fwd, fwd, fwd, bwd, bwd, bwd, fwd_d, fwd_d, fwd_d, bwd_d, bwd_d, bwd_d],
        out_specs=[fwd, bwd],
        out_shape=[out, out],
        scratch_shapes=[pltpu.VMEM((2, nt, LANES, LANES), F32)],
        compiler_params=_params("parallel", "parallel", "arbitrary"),
        name="wkv",
    )(r, v, aa, r, v, aa, lw, kd, bb, lw, kd, bb)


def _rwkv_out_kernel(yf_ref, yb_ref, bonus_ref, g_ref, lnw_ref, lnb_ref, hs_ref, wo_ref, x_ref, o_ref):
    hs = hs_ref[...]
    y = yf_ref[...] + yb_ref[...]
    parts = []
    for t in range(N_MXU_TILES):
        ys = y[:, t * MXU_W:(t + 1) * MXU_W]
        hi, lo = _split2(ys)
        mean = (_head_sum(hi, hs) + _head_sum(lo, hs)) * (1.0 / HEAD_DIM)
        dlt = ys - mean
        var = _head_sum((dlt * dlt).astype(BF16), hs) * (1.0 / HEAD_DIM)
        parts.append(dlt * lax.rsqrt(var + GN_EPS))
    yn = jnp.concatenate(parts, axis=1) * lnw_ref[...] + lnb_ref[...]
    zz = ((yn + bonus_ref[...]) * g_ref[...]).astype(BF16)
    o_ref[...] = x_ref[...] + _dot(zz, wo_ref[...])


def _rwkv_out_call(yf, yb, bonus, g, ln_w, ln_b, hs, w_o, x2, tm=256):
    n = x2.shape[0]
    tok = pl.BlockSpec((tm, D_MODEL), lambda i: (i, 0))
    return pl.pallas_call(
        _rwkv_out_kernel,
        grid=(n // tm,),
        in_specs=[tok, tok, tok, tok,
                  _const_spec((1, D_MODEL)), _const_spec((1, D_MODEL)), _const_spec((MXU_W, MXU_W)),
                  _const_spec((D_MODEL, D_MODEL)), tok],
        out_specs=tok,
        out_shape=jax.ShapeDtypeStruct((n, D_MODEL), F32),
        compiler_params=_params("parallel"),
        name="rwkv_out",
    )(yf, yb, bonus, g, ln_w, ln_b, hs, w_o, x2)


def _row(v):
    return v.reshape(1, -1).astype(F32)


def _pad_to(a, axis, size):
    pad = [(0, 0)] * a.ndim
    pad[axis] = (0, size - a.shape[axis])
    return jnp.pad(a, pad)


def _block_diag2(m):
    l, d = m.shape[1], m.shape[2]
    zero = jnp.zeros((l, d), m.dtype)
    return jnp.concatenate([jnp.concatenate([m[0], zero], axis=1), jnp.concatenate([zero, m[1]], axis=1)], axis=0)


def _round_up(n, m):
    return (n + m - 1) // m * m


def _prepare(w):
    hs = np.kron(np.eye(MXU_W // HEAD_DIM), np.ones((HEAD_DIM, HEAD_DIM)))
    hs = jnp.asarray(hs, BF16)
    na = []
    for li in range(w["na_w_qkv"].shape[0]):
        na.append(dict(
            w_qkv=w["na_w_qkv"][li].astype(BF16),
            w_o=w["na_w_o"][li].astype(BF16),
            gq=_row(jnp.tile(w["na_q_gain"][li], N_HEADS)) * (HEAD_DIM ** -0.5),
            gk=_row(jnp.tile(w["na_k_gain"][li], N_HEADS)),
            tbl=_na_bias_table(w["na_rpb"][li]),
        ))
    rw = []
    for li in range(w["rw_w_r"].shape[0]):
        lg = _round_up(w["rw_g1"].shape[-1], LANES)
        p = dict(
            mu=w["rw_mu"][li].astype(F32),
            w_r=w["rw_w_r"][li].astype(BF16), w_k=w["rw_w_k"][li].astype(BF16),
            w_v=w["rw_w_v"][li].astype(BF16), w_o=w["rw_w_o"][li].astype(BF16),
            w1=jnp.concatenate([w["rw_w1"][li, 0], w["rw_w1"][li, 1]], axis=1).astype(BF16),
            w2=_block_diag2(w["rw_w2"][li]).astype(BF16),
            a1=jnp.concatenate([w["rw_a1"][li, 0], w["rw_a1"][li, 1]], axis=1).astype(BF16),
            a2=_block_diag2(w["rw_a2"][li]).astype(BF16),
            g1=_pad_to(w["rw_g1"][li], 1, lg).astype(BF16),
            g2=_pad_to(w["rw_g2"][li], 0, lg).astype(BF16),
            w0=_row(w["rw_w0"][li]), a0=_row(w["rw_a0"][li]),
            k_k=_row(w["rw_k_k"][li]), k_a=_row(w["rw_k_a"][li]), r_k=_row(w["rw_r_k"][li]),
            ln_w=_row(w["rw_ln_w"][li]), ln_b=_row(w["rw_ln_b"][li]),
            hs=hs,
        )
        if li >= 1:
            lv = _round_up(w["rw_v1"].shape[-1], LANES)
            p.update(v0=_row(w["rw_v0"][li - 1]),
                     v1=_pad_to(w["rw_v1"][li - 1], 1, lv).astype(BF16),
                     v2=_pad_to(w["rw_v2"][li - 1], 0, lv).astype(BF16))
        rw.append(p)
    ffn = []
    for i in range(DEPTH):
        ffn.append(dict(
            w_gate=w["ffn_w_gate"][i].astype(BF16), w_up=w["ffn_w_up"][i].astype(BF16),
            conv_w=w["ffn_conv_w"][i].astype(F32), conv_b=_row(w["ffn_conv_b"][i]),
            w_down=w["ffn_w_down"][i].astype(BF16),
        ))
    return dict(na=na, rw=rw, ffn=ffn, hs=hs,
                norm_mix=w["norm_mix"].astype(F32), norm_ffn=w["norm_ffn"].astype(F32))


def _trunk(x, p):
    b, t, d = x.shape
    n = b * t
    x2 = x.reshape(n, d)
    v_first = None
    for i in range(DEPTH):
        li = i // 2
        g_mix = _row(p["norm_mix"][i])
        if i % 2 == 0:
            a = p["na"][li]
            q, k, v = _qkv_call(x2, g_mix, a["w_qkv"], a["gq"], a["gk"], p["hs"])
            o = _na_call(q.reshape(b, t, d), k.reshape(b, t, d), v.reshape(b, t, d), a["tbl"])
            x2 = _mm_res_call(o.reshape(n, d), a["w_o"], x2)
        else:
            rp = dict(p["rw"][li], g=g_mix)
            r, v, aa, g, bonus, lw, kd, bb = _rwkv_in_call(x2, t, rp, v_first)
            if v_first is None:
                v_first = v
            yf, yb = _wkv_call(r.reshape(b, t, d), v.reshape(b, t, d), aa.reshape(b, t, d),
                               lw.reshape(2, b, t, d), kd.reshape(2, b, t, d), bb.reshape(2, b, t, d))
            x2 = _rwkv_out_call(yf.reshape(n, d), yb.reshape(n, d), bonus, g, rp["ln_w"], rp["ln_b"], p["hs"],
                                rp["w_o"], x2)
        f = p["ffn"][i]
        gt, up = _ffn_in_call(x2, _row(p["norm_ffn"][i]), f["w_gate"], f["w_up"])
        x2 = _ffn_out_call(gt, up, f["conv_w"], f["conv_b"], f["w_down"], x2, t)
    return x2.reshape(b, t, d)


def kernel(x_prompt, x_sample, norm_mix, norm_ffn, na_w_qkv, na_w_o, na_q_gain, na_k_gain, na_rpb, rw_mu, rw_w_r, rw_w_k, rw_w_v, rw_w_o, rw_w0, rw_w1, rw_w2, rw_a0, rw_a1, rw_a2, rw_v0, rw_v1, rw_v2, rw_g1, rw_g2, rw_k_k, rw_k_a, rw_r_k, rw_ln_w, rw_ln_b, ffn_w_gate, ffn_w_up, ffn_conv_w, ffn_conv_b, ffn_w_down):
    w = dict(norm_mix=norm_mix, norm_ffn=norm_ffn, na_w_qkv=na_w_qkv, na_w_o=na_w_o, na_q_gain=na_q_gain,
             na_k_gain=na_k_gain, na_rpb=na_rpb, rw_mu=rw_mu, rw_w_r=rw_w_r, rw_w_k=rw_w_k, rw_w_v=rw_w_v,
             rw_w_o=rw_w_o, rw_w0=rw_w0, rw_w1=rw_w1, rw_w2=rw_w2, rw_a0=rw_a0, rw_a1=rw_a1, rw_a2=rw_a2,
             rw_v0=rw_v0, rw_v1=rw_v1, rw_v2=rw_v2, rw_g1=rw_g1, rw_g2=rw_g2, rw_k_k=rw_k_k, rw_k_a=rw_k_a,
             rw_r_k=rw_r_k, rw_ln_w=rw_ln_w, rw_ln_b=rw_ln_b, ffn_w_gate=ffn_w_gate, ffn_w_up=ffn_w_up,
             ffn_conv_w=ffn_conv_w, ffn_conv_b=ffn_conv_b, ffn_w_down=ffn_w_down)
    p = _prepare(w)
    return (_trunk(x_prompt, p), _trunk(x_sample, p))
```

```python
import functools
import math

import numpy as np
import jax
import jax.numpy as jnp
from jax import lax
from jax.experimental import pallas as pl
from jax.experimental.pallas import tpu as pltpu

F32 = jnp.float32
BF16 = jnp.bfloat16

D_MODEL = 1024
GRID_W = 64
N_HEADS = 16
HEAD_DIM = 64
WIN_H = 8
WIN_W = 16
RPB_H = 2 * WIN_H - 1
RPB_W = 2 * WIN_W - 1
D_FF = 2816
DEPTH = 4
RMS_EPS = 1e-6
GN_EPS = 64e-5
NEG_INF = -1e30

LANES = 128
HEADS_PER_TILE = LANES // HEAD_DIM
N_HEAD_TILES = D_MODEL // LANES
MXU_W = 256
N_MXU_TILES = D_MODEL // MXU_W
F32_SUBLANES = 8
FF_CHUNK = MXU_W
CHUNK = 64
NA_ROWS_PER_STEP = 8
VMEM_LIMIT = 56 * 1024 * 1024


def _params(*sem):
    return pltpu.CompilerParams(dimension_semantics=sem, vmem_limit_bytes=VMEM_LIMIT)


def _const_spec(shape):
    nd = len(shape)
    return pl.BlockSpec(shape, lambda *_: (0,) * nd)


def _dot(a, b):
    return jnp.dot(a, b, preferred_element_type=F32)


def _dot_nt(a, b):
    return lax.dot_general(a, b, (((1,), (1,)), ((), ())), preferred_element_type=F32)


def _rms_rows(xf, g):
    ms = jnp.mean(xf * xf, axis=-1, keepdims=True)
    return xf * lax.rsqrt(ms + RMS_EPS) * g


def _head_sum(x_bf16, hsum):
    return _dot(x_bf16, hsum)


def _sigmoid(x):
    return 0.5 * jnp.tanh(0.5 * x) + 0.5


def _split2(x):
    hi = x.astype(BF16)
    lo = (x - hi.astype(F32)).astype(BF16)
    return hi, lo


def _split3(x):
    hi = x.astype(BF16)
    r1 = x - hi.astype(F32)
    mid = r1.astype(BF16)
    lo = (r1 - mid.astype(F32)).astype(BF16)
    return hi, mid, lo


def _qkv_kernel(x_ref, g_ref, w_ref, gq_ref, gk_ref, hs_ref, q_ref, k_ref, v_ref):
    hn = _rms_rows(x_ref[...], g_ref[...]).astype(BF16)
    hs = hs_ref[...]
    for part, (o_ref, gain_ref) in enumerate(((q_ref, gq_ref), (k_ref, gk_ref))):
        y = _dot(hn, w_ref[:, part * D_MODEL:(part + 1) * D_MODEL])
        gain = gain_ref[...]
        for t in range(N_MXU_TILES):
            sl = slice(t * MXU_W, (t + 1) * MXU_W)
            ys = y[:, sl]
            ms = _head_sum((ys * ys).astype(BF16), hs) * (1.0 / HEAD_DIM)
            o_ref[:, sl] = (ys * lax.rsqrt(ms + RMS_EPS) * gain[:, sl]).astype(BF16)
    v_ref[...] = _dot(hn, w_ref[:, 2 * D_MODEL:]).astype(BF16)


def _qkv_call(x2, g, w_qkv, gq, gk, hs, tm=512):
    n = x2.shape[0]
    tok = pl.BlockSpec((tm, D_MODEL), lambda i: (i, 0))
    out = jax.ShapeDtypeStruct((n, D_MODEL), BF16)
    return pl.pallas_call(
        _qkv_kernel,
        grid=(n // tm,),
        in_specs=[tok, _const_spec((1, D_MODEL)), _const_spec((D_MODEL, 3 * D_MODEL)),
                  _const_spec((1, D_MODEL)), _const_spec((1, D_MODEL)), _const_spec((MXU_W, MXU_W))],
        out_specs=[tok, tok, tok],
        out_shape=[out, out, out],
        compiler_params=_params("parallel"),
        name="na_qkv",
    )(x2, g, w_qkv, gq, gk, hs)


def _na_kernel(q_ref, k_ref, v_ref, tbl_ref, o_ref, *, rows):
    kh = min(WIN_H, rows)
    lane = lax.broadcasted_iota(jnp.int32, (GRID_W, LANES), 1)
    head_mask = [lane < HEAD_DIM, lane >= HEAD_DIM]

    def rows_body(rb, carry):
        st = []
        for ri in range(NA_ROWS_PER_STEP):
            r = rb * NA_ROWS_PER_STEP + ri
            rs = jnp.clip(r - kh // 2, 0, rows - kh)
            qsl = pl.ds(pl.multiple_of(r * GRID_W, GRID_W), GRID_W)
            ksl = pl.ds(pl.multiple_of(rs * GRID_W, GRID_W), kh * GRID_W)
            q = q_ref[0, qsl, :]
            zero = jnp.zeros_like(q)
            qs = jnp.concatenate([jnp.where(head_mask[0], q, zero), jnp.where(head_mask[1], q, zero)], axis=0)
            st.append(dict(dr0=rs - r + WIN_H - 1, ksl=ksl, qsl=qsl, s=_dot_nt(qs, k_ref[0, ksl, :])))
        for e in st:
            bias = jnp.concatenate([tbl_ref[0, e["dr0"] + 2 * p] for p in range(kh // 2)], axis=1)
            s = e.pop("s") + bias
            m = jnp.max(s, axis=1, keepdims=True)
            p = jnp.exp(s - m)
            e["l"] = jnp.sum(p, axis=1, keepdims=True)
            e["p"] = p.astype(BF16)
        for e in st:
            e["o"] = _dot(e.pop("p"), v_ref[0, e["ksl"], :]) / e["l"]
        for e in st:
            o = e["o"]
            o_ref[0, e["qsl"], :] = jnp.where(head_mask[0], o[:GRID_W], o[GRID_W:]).astype(BF16)
        return carry

    lax.fori_loop(0, rows // NA_ROWS_PER_STEP, rows_body, 0)


def _na_call(q, k, v, tbl):
    b, t, _ = q.shape
    rows = t // GRID_W
    assert rows >= WIN_H and WIN_H % 2 == 0
    blk = pl.BlockSpec((1, t, LANES), lambda bi, hi: (bi, 0, hi))
    return pl.pallas_call(
        functools.partial(_na_kernel, rows=rows),
        grid=(b, N_HEAD_TILES),
        in_specs=[blk, blk, blk,
                  pl.BlockSpec((1, RPB_H - 1, HEADS_PER_TILE * GRID_W, LANES), lambda bi, hi: (hi, 0, 0, 0))],
        out_specs=blk,
        out_shape=jax.ShapeDtypeStruct((b, t, D_MODEL), BF16),
        compiler_params=_params("parallel", "parallel"),
        name="na_attn",
    )(q, k, v, tbl)


def _na_bias_table(rpb):
    c = np.arange(GRID_W)[:, None]
    kc = np.arange(GRID_W)[None, :]
    cs = np.clip(c - WIN_W // 2, 0, GRID_W - WIN_W)
    ok = (kc >= cs) & (kc < cs + WIN_W)
    d_col = np.clip(kc - c + WIN_W - 1, 0, RPB_W - 1)
    base = jnp.where(ok[None, None], rpb.astype(F32)[:, :, d_col], NEG_INF)
    pairs = jnp.concatenate([base[:, :-1], base[:, 1:]], axis=-1)
    pairs = pairs.reshape(N_HEAD_TILES, HEADS_PER_TILE, RPB_H - 1, GRID_W, LANES)
    return jnp.transpose(pairs, (0, 2, 1, 3, 4)).reshape(N_HEAD_TILES, RPB_H - 1, HEADS_PER_TILE * GRID_W, LANES)


def _mm_res_kernel(a_ref, w_ref, x_ref, o_ref):
    o_ref[...] = x_ref[...] + _dot(a_ref[...], w_ref[...])


def _mm_res_call(a, w, x2, tm=512):
    n, kdim = a.shape
    tok = pl.BlockSpec((tm, D_MODEL), lambda i: (i, 0))
    return pl.pallas_call(
        _mm_res_kernel,
        grid=(n // tm,),
        in_specs=[pl.BlockSpec((tm, kdim), lambda i: (i, 0)), _const_spec((kdim, D_MODEL)), tok],
        out_specs=tok,
        out_shape=jax.ShapeDtypeStruct((n, D_MODEL), F32),
        compiler_params=_params("parallel"),
        name="mm_res",
    )(a, w, x2)


def _shift_rows(cur, prev_row, next_row, tm):
    rid = lax.broadcasted_iota(jnp.int32, (F32_SUBLANES, 1), 0)
    down = pltpu.roll(cur, 1, axis=0)
    up = pltpu.roll(cur, tm - 1, axis=0)
    prev = jnp.concatenate([jnp.where(rid == 0, prev_row, down[:F32_SUBLANES]), down[F32_SUBLANES:]], axis=0)
    nxt = jnp.concatenate([up[:tm - F32_SUBLANES],
                           jnp.where(rid == F32_SUBLANES - 1, next_row, up[tm - F32_SUBLANES:])], axis=0)
    return prev, nxt


def _ffn_kernel(x_ref, xp_ref, xn_ref, g_ref, wg_ref, wu_ref, cw_ref, cb_ref, wd_ref, o_ref, *, tm, seq):
    t0 = pl.program_id(0) * tm
    first = (t0 % seq) == 0
    last = ((t0 + tm) % seq) == 0
    gn = g_ref[...]
    x = x_ref[...]
    h_mid = _rms_rows(x, gn)
    h_prev = jnp.where(first, 0.0, _rms_rows(xp_ref[...], gn))
    h_next = jnp.where(last, 0.0, _rms_rows(xn_ref[...], gn))
    hn_ext = jnp.concatenate([h_prev, h_mid, h_next], axis=0).astype(BF16)
    hn = h_mid.astype(BF16)
    s = F32_SUBLANES
    ext = tm + 2 * s
    cw = 0.5 * cw_ref[...]
    cb = 0.5 * cb_ref[...]

    def gate_up(c):
        cs = slice(c * FF_CHUNK, (c + 1) * FF_CHUNK)
        return _dot(hn_ext, wg_ref[:, cs]), _dot(hn, wu_ref[:, cs])

    acc = x
    nxt = gate_up(0)
    for c in range(D_FF // FF_CHUNK):
        cs = slice(c * FF_CHUNK, (c + 1) * FF_CHUNK)
        ge, up = nxt
        if c + 1 < D_FF // FF_CHUNK:
            nxt = gate_up(c + 1)
        gprev = pltpu.roll(ge, 1, axis=0)[s:s + tm]
        gnext = pltpu.roll(ge, ext - 1, axis=0)[s:s + tm]
        h = gprev * cw[0:1, cs] + ge[s:s + tm] * cw[1:2, cs] + gnext * cw[2:3, cs] + cb[:, cs]
        act = (h * (jnp.tanh(h) + 1.0) * up).astype(BF16)
        acc = acc + _dot(act, wd_ref[cs, :])
    o_ref[...] = acc


def _ffn_call(x2, g, w_gate, w_up, conv_w, conv_b, w_down, seq, tm=256):
    n = x2.shape[0]
    hb = tm // F32_SUBLANES
    nhb = n // F32_SUBLANES
    tok = pl.BlockSpec((tm, D_MODEL), lambda i: (i, 0))
    halo_p = pl.BlockSpec((F32_SUBLANES, D_MODEL), lambda i: (jnp.maximum(i * hb - 1, 0), 0))
    halo_n = pl.BlockSpec((F32_SUBLANES, D_MODEL), lambda i: (jnp.minimum((i + 1) * hb, nhb - 1), 0))
    return pl.pallas_call(
        functools.partial(_ffn_kernel, tm=tm, seq=seq),
        grid=(n // tm,),
        in_specs=[tok, halo_p, halo_n, _const_spec((1, D_MODEL)), _const_spec((D_MODEL, D_FF)),
                  _const_spec((D_MODEL, D_FF)), _const_spec((3, D_FF)), _const_spec((1, D_FF)),
                  _const_spec((D_FF, D_MODEL))],
        out_specs=tok,
        out_shape=jax.ShapeDtypeStruct((n, D_MODEL), F32),
        compiler_params=_params("parallel"),
        name="ffn",
    )(x2, x2, x2, g, w_gate, w_up, conv_w, conv_b, w_down)


def _rwkv_in_kernel(*refs, tm, seq, has_vres):
    (x_ref, xp_ref, xn_ref, g_ref, mu_ref, wr_ref, wk_ref, wv_ref, w1_ref, w2_ref, a1_ref, a2_ref,
     g1_ref, g2_ref, w0_ref, a0_ref, kk_ref, ka_ref, rk_ref, hs_ref) = refs[:20]
    refs = refs[20:]
    if has_vres:
        vf_ref, v0_ref, v1_ref, v2_ref = refs[:4]
        refs = refs[4:]
    r_out, v_out, aa_out, g_out, bonus_out, lw_out, kd_out, bb_out = refs

    t0 = pl.program_id(0) * tm
    first = (t0 % seq) == 0
    last = ((t0 + tm) % seq) == 0
    gn = g_ref[...]
    h = _rms_rows(x_ref[...], gn)
    prev_row = jnp.where(first, 0.0, _rms_rows(xp_ref[...], gn)[F32_SUBLANES - 1:F32_SUBLANES, :])
    next_row = jnp.where(last, 0.0, _rms_rows(xn_ref[...], gn)[0:1, :])
    prev, nxt = _shift_rows(h, prev_row, next_row, tm)
    xx = 0.5 * (prev + nxt) - h
    mu = mu_ref[...]
    xr, xw, xk, xv, xa, xg = ((h + xx * mu[i:i + 1, :]).astype(BF16) for i in range(6))

    r = _dot(xr, wr_ref[...])
    k = _dot(xk, wk_ref[...])
    v = _dot(xv, wv_ref[...])
    if has_vres:
        vl = _dot(_dot(xv, v1_ref[...]).astype(BF16), v2_ref[...])
        v = v + (vf_ref[...] - v) * _sigmoid(v0_ref[...] + vl)
    g_out[...] = _dot(_sigmoid(_dot(xg, g1_ref[...])).astype(BF16), g2_ref[...])
    lwl = _dot(jnp.tanh(_dot(xw, w1_ref[...])).astype(BF16), w2_ref[...])
    la = _dot(_dot(xa, a1_ref[...]).astype(BF16), a2_ref[...])

    hs = hs_ref[...]
    kk = k * kk_ref[...]
    kkn_parts = []
    for t in range(N_MXU_TILES):
        ks = kk[:, t * MXU_W:(t + 1) * MXU_W]
        ss = _head_sum((ks * ks).astype(BF16), hs)
        kkn_parts.append(ks * jnp.minimum(lax.rsqrt(ss), 1e12))
    kkn = jnp.concatenate(kkn_parts, axis=1)

    ka = ka_ref[...]
    kd_sum = None
    for z in range(2):
        zs = slice(z * D_MODEL, (z + 1) * D_MODEL)
        lw_out[z] = (-math.exp(-0.5)) * _sigmoid(w0_ref[:, zs] + lwl[:, zs])
        a = _sigmoid(a0_ref[:, zs] + la[:, zs])
        kd = k * (1.0 + (a - 1.0) * ka)
        kd_out[z] = kd
        bb_out[z] = kkn * a
        kd_sum = kd if kd_sum is None else kd_sum + kd

    rkk = r * kd_sum * rk_ref[...]
    for t in range(N_MXU_TILES):
        sl = slice(t * MXU_W, (t + 1) * MXU_W)
        hi, lo = _split2(rkk[:, sl])
        bonus_out[:, sl] = (_head_sum(hi, hs) + _head_sum(lo, hs)) * v[:, sl]
    r_out[...] = r
    v_out[...] = v
    aa_out[...] = -kkn


def _rwkv_in_call(x2, seq, p, v_first, tm=256):
    n = x2.shape[0]
    has_vres = v_first is not None
    hb = tm // F32_SUBLANES
    nhb = n // F32_SUBLANES
    tok = pl.BlockSpec((tm, D_MODEL), lambda i: (i, 0))
    tok2 = pl.BlockSpec((2, tm, D_MODEL), lambda i: (0, i, 0))
    halo_p = pl.BlockSpec((F32_SUBLANES, D_MODEL), lambda i: (jnp.maximum(i * hb - 1, 0), 0))
    halo_n = pl.BlockSpec((F32_SUBLANES, D_MODEL), lambda i: (jnp.minimum((i + 1) * hb, nhb - 1), 0))
    args = [x2, x2, x2, p["g"], p["mu"], p["w_r"], p["w_k"], p["w_v"], p["w1"], p["w2"], p["a1"], p["a2"],
            p["g1"], p["g2"], p["w0"], p["a0"], p["k_k"], p["k_a"], p["r_k"], p["hs"]]
    specs = [tok, halo_p, halo_n] + [_const_spec(a.shape) for a in args[3:]]
    if has_vres:
        extra = [v_first, p["v0"], p["v1"], p["v2"]]
        args += extra
        specs += [tok] + [_const_spec(a.shape) for a in extra[1:]]
    o1 = jax.ShapeDtypeStruct((n, D_MODEL), F32)
    o2 = jax.ShapeDtypeStruct((2, n, D_MODEL), F32)
    return pl.pallas_call(
        functools.partial(_rwkv_in_kernel, tm=tm, seq=seq, has_vres=has_vres),
        grid=(n // tm,),
        in_specs=specs,
        out_specs=[tok] * 5 + [tok2] * 3,
        out_shape=[o1] * 5 + [o2] * 3,
        compiler_params=_params("parallel"),
        name="rwkv_in",
    )(*args)


def _wkv_kernel(rf_ref, vf_ref, af_ref, rb_ref, vb_ref, ab_ref, lwf_ref, kf_ref, bf_ref, lwb_ref, kb_ref, bb_ref,
                yf_ref, yb_ref, s_ref, *, tb, nt):
    assert CHUNK == HEAD_DIM
    nchunk = tb // CHUNK
    c1 = CHUNK

    @pl.when(pl.program_id(2) == 0)
    def _():
        s_ref[...] = jnp.zeros_like(s_ref)

    lane = lax.broadcasted_iota(jnp.int32, (c1, LANES), 1)
    d = lax.broadcasted_iota(jnp.int32, (c1, LANES), 0) - (lane & (c1 - 1))
    rid = lax.broadcasted_iota(jnp.int32, (c1, 1), 0)
    eye = (d == 0).astype(F32)
    m0 = lane < HEAD_DIM
    m0w = jnp.concatenate([m0, m0], axis=1)
    dirs = (
        (rf_ref, vf_ref, af_ref, lwf_ref, kf_ref, bf_ref, yf_ref, d > 0, d >= 0, False),
        (rb_ref, vb_ref, ab_ref, lwb_ref, kb_ref, bb_ref, yb_ref, d < 0, d <= 0, True),
    )

    def running_sum(x, backward):
        sh = 1
        while sh < c1:
            if backward:
                x = x + jnp.where(rid < c1 - sh, pltpu.roll(x, c1 - sh, axis=0), 0.0)
            else:
                x = x + jnp.where(rid >= sh, pltpu.roll(x, sh, axis=0), 0.0)
            sh *= 2
        return x

    def stack(x, mask=m0):
        zero = jnp.zeros_like(x)
        return jnp.concatenate([jnp.where(mask, x, zero), jnp.where(mask, zero, x)], axis=0)

    def chunk_body(ci, carry):
        st = []
        for z, (r_ref, v_ref, a_ref, lw_ref, k_ref, b_ref, y_ref, strict, incl, backward) in enumerate(dirs):
            c = ci if z == 0 else nchunk - 1 - ci
            sl = pl.ds(pl.multiple_of(c * CHUNK, CHUNK), CHUNK)
            lw_all = lw_ref[0, 0, sl, :]
            cl_all = running_sum(lw_all, backward)
            for t in range(nt):
                ls = slice(t * LANES, (t + 1) * LANES)
                lw = lw_all[:, ls]
                cl = cl_all[:, ls]
                e_in = jnp.exp(cl)
                e_ex = jnp.exp(cl - lw)
                e_neg = jnp.exp(-cl)
                q = dict(zi=z, t=t, sl=sl, ls=ls, y_ref=y_ref, strict=strict, incl=incl)
                q["wc"] = jnp.exp(jnp.sum(lw, axis=0, keepdims=True))
                q["ar"] = jnp.concatenate([a_ref[0, sl, ls] * e_ex, r_ref[0, sl, ls] * e_in], axis=0).astype(BF16)
                q["bk"] = jnp.concatenate([stack((b_ref[0, 0, sl, ls] * e_neg).astype(BF16)),
                                           stack((k_ref[0, 0, sl, ls] * e_neg).astype(BF16))], axis=0)
                q["vs"] = stack(v_ref[0, sl, ls].astype(BF16))
                st.append(q)
        for q in st:
            q["g"] = _dot_nt(q["ar"], q["bk"])
        for q in st:
            g = q.pop("g")
            a_ab = jnp.where(q["strict"], g[:c1, :LANES], 0.0)
            q["a_ak"] = jnp.where(q["strict"], g[:c1, LANES:], 0.0).astype(BF16)
            q["a_r"] = jnp.concatenate([jnp.where(q["incl"], g[c1:, :LANES], 0.0),
                                        jnp.where(q["incl"], g[c1:, LANES:], 0.0)], axis=1).astype(BF16)
            q["tm"] = eye + a_ab
            q["pw"] = a_ab.astype(BF16)
        levels = CHUNK.bit_length() - 1
        for q in st:
            q["pw"] = _dot(q["pw"], stack(q["pw"])).astype(BF16)
        for _ in range(levels - 2):
            for q in st:
                pt = jnp.concatenate([q["pw"], q["tm"].astype(BF16)], axis=0)
                q["z"] = _dot(pt, stack(q["pw"]))
            for q in st:
                z = q.pop("z")
                q["pw"] = z[:c1].astype(BF16)
                q["tm"] = q["tm"] + z[c1:]
        for q in st:
            q["tm"] = q["tm"] + _dot(q["pw"], stack(q["tm"].astype(BF16)))
        for q in st:
            q["s"] = s_ref[q["zi"], q["t"]]
            xy = _dot_nt(q["ar"], q["s"].astype(BF16))
            q["y_s"] = xy[c1:]
            q["x"] = xy[:c1] + _dot(q["a_ak"], q["vs"])
        for q in st:
            q["u"] = stack(_dot(q["tm"].astype(BF16), stack(q["x"].astype(BF16))))
        for q in st:
            uv = jnp.concatenate([q["u"].astype(BF16), q["vs"]], axis=0)
            q["y_ref"][0, q["sl"], q["ls"]] = q["y_s"] + _dot(q["a_r"], uv)
        for q in st:
            uv_t = jnp.concatenate([q["u"].T, q["vs"].astype(F32).T], axis=1).astype(BF16)
            s_ref[q["zi"], q["t"]] = (q["s"] + _dot(uv_t, q["bk"])) * q["wc"]
        return carry

    lax.fori_loop(0, nchunk, chunk_body, 0, unroll=2)


def _wkv_call(r, v, aa, lw, kd, bb, tb=512, nt=4):
    b, t, _ = r.shape
    tb = min(tb, t)
    ntb = t // tb
    w = nt * LANES
    fwd = pl.BlockSpec((1, tb, w), lambda bi, hi, j: (bi, j, hi))
    bwd = pl.BlockSpec((1, tb, w), lambda bi, hi, j: (bi, ntb - 1 - j, hi))
    fwd_d = pl.BlockSpec((1, 1, tb, w), lambda bi, hi, j: (0, bi, j, hi))
    bwd_d = pl.BlockSpec((1, 1, tb, w), lambda bi, hi, j: (1, bi, ntb - 1 - j, hi))
    out = jax.ShapeDtypeStruct((b, t, D_MODEL), F32)
    return pl.pallas_call(
        functools.partial(_wkv_kernel, tb=tb, nt=nt),
        grid=(b, N_HEAD_TILES // nt, ntb),
        in_specs=[fwd, fwd, fwd, bwd, bwd, bwd, fwd_d, fwd_d, fwd_d, bwd_d, bwd_d, bwd_d],
        out_specs=[fwd, bwd],
        out_shape=[out, out],
        scratch_shapes=[pltpu.VMEM((2, nt, LANES, LANES), F32)],
        compiler_params=_params("parallel", "parallel", "arbitrary"),
        name="wkv",
    )(r, v, aa, r, v, aa, lw, kd, bb, lw, kd, bb)


def _rwkv_out_kernel(yf_ref, yb_ref, bonus_ref, g_ref, lnw_ref, lnb_ref, hs_ref, wo_ref, x_ref, o_ref):
    hs = hs_ref[...]
    y = yf_ref[...] + yb_ref[...]
    parts = []
    for t in range(N_MXU_TILES):
        ys = y[:, t * MXU_W:(t + 1) * MXU_W]
        hi, lo = _split2(ys)
        mean = (_head_sum(hi, hs) + _head_sum(lo, hs)) * (1.0 / HEAD_DIM)
        dlt = ys - mean
        var = _head_sum((dlt * dlt).astype(BF16), hs) * (1.0 / HEAD_DIM)
        parts.append(dlt * lax.rsqrt(var + GN_EPS))
    yn = jnp.concatenate(parts, axis=1) * lnw_ref[...] + lnb_ref[...]
    zz = ((yn + bonus_ref[...]) * g_ref[...]).astype(BF16)
    o_ref[...] = x_ref[...] + _dot(zz, wo_ref[...])


def _rwkv_out_call(yf, yb, bonus, g, ln_w, ln_b, hs, w_o, x2, tm=512):
    n = x2.shape[0]
    tok = pl.BlockSpec((tm, D_MODEL), lambda i: (i, 0))
    return pl.pallas_call(
        _rwkv_out_kernel,
        grid=(n // tm,),
        in_specs=[tok, tok, tok, tok,
                  _const_spec((1, D_MODEL)), _const_spec((1, D_MODEL)), _const_spec((MXU_W, MXU_W)),
                  _const_spec((D_MODEL, D_MODEL)), tok],
        out_specs=tok,
        out_shape=jax.ShapeDtypeStruct((n, D_MODEL), F32),
        compiler_params=_params("parallel"),
        name="rwkv_out",
    )(yf, yb, bonus, g, ln_w, ln_b, hs, w_o, x2)


def _row(v):
    return v.reshape(1, -1).astype(F32)


def _pad_to(a, axis, size):
    pad = [(0, 0)] * a.ndim
    pad[axis] = (0, size - a.shape[axis])
    return jnp.pad(a, pad)


def _block_diag2(m):
    l, d = m.shape[1], m.shape[2]
    zero = jnp.zeros((l, d), m.dtype)
    return jnp.concatenate([jnp.concatenate([m[0], zero], axis=1), jnp.concatenate([zero, m[1]], axis=1)], axis=0)


def _round_up(n, m):
    return (n + m - 1) // m * m


def _prepare(w):
    hs = np.kron(np.eye(MXU_W // HEAD_DIM), np.ones((HEAD_DIM, HEAD_DIM)))
    hs = jnp.asarray(hs, BF16)
    na = []
    for li in range(w["na_w_qkv"].shape[0]):
        na.append(dict(
            w_qkv=w["na_w_qkv"][li].astype(BF16),
            w_o=w["na_w_o"][li].astype(BF16),
            gq=_row(jnp.tile(w["na_q_gain"][li], N_HEADS)) * (HEAD_DIM ** -0.5),
            gk=_row(jnp.tile(w["na_k_gain"][li], N_HEADS)),
            tbl=_na_bias_table(w["na_rpb"][li]),
        ))
    rw = []
    for li in range(w["rw_w_r"].shape[0]):
        lg = _round_up(w["rw_g1"].shape[-1], LANES)
        p = dict(
            mu=w["rw_mu"][li].astype(F32),
            w_r=w["rw_w_r"][li].astype(BF16), w_k=w["rw_w_k"][li].astype(BF16),
            w_v=w["rw_w_v"][li].astype(BF16), w_o=w["rw_w_o"][li].astype(BF16),
            w1=jnp.concatenate([w["rw_w1"][li, 0], w["rw_w1"][li, 1]], axis=1).astype(BF16),
            w2=_block_diag2(w["rw_w2"][li]).astype(BF16),
            a1=jnp.concatenate([w["rw_a1"][li, 0], w["rw_a1"][li, 1]], axis=1).astype(BF16),
            a2=_block_diag2(w["rw_a2"][li]).astype(BF16),
            g1=_pad_to(w["rw_g1"][li], 1, lg).astype(BF16),
            g2=_pad_to(w["rw_g2"][li], 0, lg).astype(BF16),
            w0=_row(w["rw_w0"][li]), a0=_row(w["rw_a0"][li]),
            k_k=_row(w["rw_k_k"][li]), k_a=_row(w["rw_k_a"][li]), r_k=_row(w["rw_r_k"][li]),
            ln_w=_row(w["rw_ln_w"][li]), ln_b=_row(w["rw_ln_b"][li]),
            hs=hs,
        )
        if li >= 1:
            lv = _round_up(w["rw_v1"].shape[-1], LANES)
            p.update(v0=_row(w["rw_v0"][li - 1]),
                     v1=_pad_to(w["rw_v1"][li - 1], 1, lv).astype(BF16),
                     v2=_pad_to(w["rw_v2"][li - 1], 0, lv).astype(BF16))
        rw.append(p)
    ffn = []
    for i in range(DEPTH):
        ffn.append(dict(
            w_gate=w["ffn_w_gate"][i].astype(BF16), w_up=w["ffn_w_up"][i].astype(BF16),
            conv_w=w["ffn_conv_w"][i].astype(F32), conv_b=_row(w["ffn_conv_b"][i]),
            w_down=w["ffn_w_down"][i].astype(BF16),
        ))
    return dict(na=na, rw=rw, ffn=ffn, hs=hs,
                norm_mix=w["norm_mix"].astype(F32), norm_ffn=w["norm_ffn"].astype(F32))


def _trunk(x, p):
    b, t, d = x.shape
    n = b * t
    x2 = x.reshape(n, d)
    v_first = None
    for i in range(DEPTH):
        li = i // 2
        g_mix = _row(p["norm_mix"][i])
        if i % 2 == 0:
            a = p["na"][li]
            q, k, v = _qkv_call(x2, g_mix, a["w_qkv"], a["gq"], a["gk"], p["hs"])
            o = _na_call(q.reshape(b, t, d), k.reshape(b, t, d), v.reshape(b, t, d), a["tbl"])
            x2 = _mm_res_call(o.reshape(n, d), a["w_o"], x2)
        else:
            rp = dict(p["rw"][li], g=g_mix)
            r, v, aa, g, bonus, lw, kd, bb = _rwkv_in_call(x2, t, rp, v_first)
            if v_first is None:
                v_first = v
            yf, yb = _wkv_call(r.reshape(b, t, d), v.reshape(b, t, d), aa.reshape(b, t, d),
                               lw.reshape(2, b, t, d), kd.reshape(2, b, t, d), bb.reshape(2, b, t, d))
            x2 = _rwkv_out_call(yf.reshape(n, d), yb.reshape(n, d), bonus, g, rp["ln_w"], rp["ln_b"], p["hs"],
                                rp["w_o"], x2)
        f = p["ffn"][i]
        x2 = _ffn_call(x2, _row(p["norm_ffn"][i]), f["w_gate"], f["w_up"], f["conv_w"], f["conv_b"], f["w_down"], t)
    return x2.reshape(b, t, d)


def kernel(x_prompt, x_sample, norm_mix, norm_ffn, na_w_qkv, na_w_o, na_q_gain, na_k_gain, na_rpb, rw_mu, rw_w_r, rw_w_k, rw_w_v, rw_w_o, rw_w0, rw_w1, rw_w2, rw_a0, rw_a1, rw_a2, rw_v0, rw_v1, rw_v2, rw_g1, rw_g2, rw_k_k, rw_k_a, rw_r_k, rw_ln_w, rw_ln_b, ffn_w_gate, ffn_w_up, ffn_conv_w, ffn_conv_b, ffn_w_down):
    w = dict(norm_mix=norm_mix, norm_ffn=norm_ffn, na_w_qkv=na_w_qkv, na_w_o=na_w_o, na_q_gain=na_q_gain,
             na_k_gain=na_k_gain, na_rpb=na_rpb, rw_mu=rw_mu, rw_w_r=rw_w_r, rw_w_k=rw_w_k, rw_w_v=rw_w_v,
             rw_w_o=rw_w_o, rw_w0=rw_w0, rw_w1=rw_w1, rw_w2=rw_w2, rw_a0=rw_a0, rw_a1=rw_a1, rw_a2=rw_a2,
             rw_v0=rw_v0, rw_v1=rw_v1, rw_v2=rw_v2, rw_g1=rw_g1, rw_g2=rw_g2, rw_k_k=rw_k_k, rw_k_a=rw_k_a,
             rw_r_k=rw_r_k, rw_ln_w=rw_ln_w, rw_ln_b=rw_ln_b, ffn_w_gate=ffn_w_gate, ffn_w_up=ffn_w_up,
             ffn_conv_w=ffn_conv_w, ffn_conv_b=ffn_conv_b, ffn_w_down=ffn_w_down)
    p = _prepare(w)
    return (_trunk(x_prompt, p), _trunk(x_sample, p))
```

```python
import functools
import math

import numpy as np
import jax
import jax.numpy as jnp
from jax import lax
from jax.experimental import pallas as pl
from jax.experimental.pallas import tpu as pltpu

F32 = jnp.float32
BF16 = jnp.bfloat16

D_MODEL = 1024
GRID_W = 64
N_HEADS = 16
HEAD_DIM = 64
WIN_H = 8
WIN_W = 16
RPB_H = 2 * WIN_H - 1
RPB_W = 2 * WIN_W - 1
D_FF = 2816
DEPTH = 4
RMS_EPS = 1e-6
GN_EPS = 64e-5
NEG_INF = -1e30

LANES = 128
HEADS_PER_TILE = LANES // HEAD_DIM
N_HEAD_TILES = D_MODEL // LANES
MXU_W = 256
N_MXU_TILES = D_MODEL // MXU_W
F32_SUBLANES = 8
HALO = 16
FF_CHUNK = MXU_W
CHUNK = 64
RWKV_IN_SUBTILES = 1
NA_ROWS_PER_STEP = 8
VMEM_LIMIT = 56 * 1024 * 1024


def _params(*sem):
    return pltpu.CompilerParams(dimension_semantics=sem, vmem_limit_bytes=VMEM_LIMIT)


def _const_spec(shape):
    nd = len(shape)
    return pl.BlockSpec(shape, lambda *_: (0,) * nd)


def _dot(a, b):
    return jnp.dot(a, b, preferred_element_type=F32)


def _dot_nt(a, b):
    return lax.dot_general(a, b, (((1,), (1,)), ((), ())), preferred_element_type=F32)


def _rms_rows(xf, g):
    ms = jnp.mean(xf * xf, axis=-1, keepdims=True)
    return xf * lax.rsqrt(ms + RMS_EPS) * g


def _head_sum(x_bf16, hsum):
    return _dot(x_bf16, hsum)


def _sigmoid(x):
    return 0.5 * jnp.tanh(0.5 * x) + 0.5


def _split2(x):
    hi = x.astype(BF16)
    lo = (x - hi.astype(F32)).astype(BF16)
    return hi, lo


def _split3(x):
    hi = x.astype(BF16)
    r1 = x - hi.astype(F32)
    mid = r1.astype(BF16)
    lo = (r1 - mid.astype(F32)).astype(BF16)
    return hi, mid, lo


def _qkv_kernel(x_ref, g_ref, w_ref, gq_ref, gk_ref, hs_ref, q_ref, k_ref, v_ref):
    hn = _rms_rows(x_ref[...], g_ref[...]).astype(BF16)
    hs = hs_ref[...]
    for part, (o_ref, gain_ref) in enumerate(((q_ref, gq_ref), (k_ref, gk_ref))):
        y = _dot(hn, w_ref[:, part * D_MODEL:(part + 1) * D_MODEL])
        gain = gain_ref[...]
        for t in range(N_MXU_TILES):
            sl = slice(t * MXU_W, (t + 1) * MXU_W)
            ys = y[:, sl]
            ms = _head_sum((ys * ys).astype(BF16), hs) * (1.0 / HEAD_DIM)
            o_ref[:, sl] = (ys * lax.rsqrt(ms + RMS_EPS) * gain[:, sl]).astype(BF16)
    v_ref[...] = _dot(hn, w_ref[:, 2 * D_MODEL:]).astype(BF16)


def _qkv_call(x2, g, w_qkv, gq, gk, hs, tm=512):
    n = x2.shape[0]
    tok = pl.BlockSpec((tm, D_MODEL), lambda i: (i, 0))
    out = jax.ShapeDtypeStruct((n, D_MODEL), BF16)
    return pl.pallas_call(
        _qkv_kernel,
        grid=(n // tm,),
        in_specs=[tok, _const_spec((1, D_MODEL)), _const_spec((D_MODEL, 3 * D_MODEL)),
                  _const_spec((1, D_MODEL)), _const_spec((1, D_MODEL)), _const_spec((MXU_W, MXU_W))],
        out_specs=[tok, tok, tok],
        out_shape=[out, out, out],
        compiler_params=_params("parallel"),
        name="na_qkv",
    )(x2, g, w_qkv, gq, gk, hs)


def _na_kernel(q_ref, k_ref, v_ref, tbl_ref, o_ref, *, rows):
    kh = min(WIN_H, rows)
    lane = lax.broadcasted_iota(jnp.int32, (GRID_W, LANES), 1)
    head_mask = [lane < HEAD_DIM, lane >= HEAD_DIM]

    def rows_body(rb, carry):
        st = []
        for ri in range(NA_ROWS_PER_STEP):
            r = rb * NA_ROWS_PER_STEP + ri
            rs = jnp.clip(r - kh // 2, 0, rows - kh)
            qsl = pl.ds(pl.multiple_of(r * GRID_W, GRID_W), GRID_W)
            ksl = pl.ds(pl.multiple_of(rs * GRID_W, GRID_W), kh * GRID_W)
            q = q_ref[0, qsl, :]
            zero = jnp.zeros_like(q)
            qs = jnp.concatenate([jnp.where(head_mask[0], q, zero), jnp.where(head_mask[1], q, zero)], axis=0)
            st.append(dict(dr0=rs - r + WIN_H - 1, ksl=ksl, qsl=qsl, s=_dot_nt(qs, k_ref[0, ksl, :])))
        for e in st:
            bias = jnp.concatenate([tbl_ref[0, e["dr0"] + 2 * p] for p in range(kh // 2)], axis=1)
            s = e.pop("s") + bias
            m = jnp.max(s, axis=1, keepdims=True)
            p = jnp.exp(s - m)
            e["l"] = jnp.sum(p, axis=1, keepdims=True)
            e["p"] = p.astype(BF16)
        for e in st:
            e["o"] = _dot(e.pop("p"), v_ref[0, e["ksl"], :]) / e["l"]
        for e in st:
            o = e["o"]
            o_ref[0, e["qsl"], :] = jnp.where(head_mask[0], o[:GRID_W], o[GRID_W:]).astype(BF16)
        return carry

    lax.fori_loop(0, rows // NA_ROWS_PER_STEP, rows_body, 0)


def _na_call(q, k, v, tbl):
    b, t, _ = q.shape
    rows = t // GRID_W
    assert rows >= WIN_H and WIN_H % 2 == 0
    blk = pl.BlockSpec((1, t, LANES), lambda bi, hi: (bi, 0, hi))
    return pl.pallas_call(
        functools.partial(_na_kernel, rows=rows),
        grid=(b, N_HEAD_TILES),
        in_specs=[blk, blk, blk,
                  pl.BlockSpec((1, RPB_H - 1, HEADS_PER_TILE * GRID_W, LANES), lambda bi, hi: (hi, 0, 0, 0))],
        out_specs=blk,
        out_shape=jax.ShapeDtypeStruct((b, t, D_MODEL), BF16),
        compiler_params=_params("parallel", "parallel"),
        name="na_attn",
    )(q, k, v, tbl)


def _na_bias_table(rpb):
    c = np.arange(GRID_W)[:, None]
    kc = np.arange(GRID_W)[None, :]
    cs = np.clip(c - WIN_W // 2, 0, GRID_W - WIN_W)
    ok = (kc >= cs) & (kc < cs + WIN_W)
    d_col = np.clip(kc - c + WIN_W - 1, 0, RPB_W - 1)
    base = jnp.where(ok[None, None], rpb.astype(F32)[:, :, d_col], NEG_INF)
    pairs = jnp.concatenate([base[:, :-1], base[:, 1:]], axis=-1)
    pairs = pairs.reshape(N_HEAD_TILES, HEADS_PER_TILE, RPB_H - 1, GRID_W, LANES)
    return jnp.transpose(pairs, (0, 2, 1, 3, 4)).reshape(N_HEAD_TILES, RPB_H - 1, HEADS_PER_TILE * GRID_W, LANES)


def _shift_rows(cur, prev_row, next_row, tm):
    rid = lax.broadcasted_iota(jnp.int32, (F32_SUBLANES, 1), 0)
    down = pltpu.roll(cur, 1, axis=0)
    up = pltpu.roll(cur, tm - 1, axis=0)
    prev = jnp.concatenate([jnp.where(rid == 0, prev_row, down[:F32_SUBLANES]), down[F32_SUBLANES:]], axis=0)
    nxt = jnp.concatenate([up[:tm - F32_SUBLANES],
                           jnp.where(rid == F32_SUBLANES - 1, next_row, up[tm - F32_SUBLANES:])], axis=0)
    return prev, nxt


def _ext(refs3):
    return jnp.concatenate([r[...] for r in refs3], axis=0)


def _ffn_kernel(*refs, tm, seq, pre):
    it = iter(refs)

    def take(k):
        return [next(it) for _ in range(k)]

    delta = None
    if pre == "proj":
        a3, (wo_ref,) = take(3), take(1)
        delta = _dot(_ext(a3), wo_ref[...])
    elif pre == "rwkv":
        yf3, yb3, bonus3, gate3 = take(3), take(3), take(3), take(3)
        lnw_ref, lnb_ref, hs_ref, wo_ref = take(4)
        zz = _rwkv_gate(_ext(yf3) + _ext(yb3), _ext(bonus3), _ext(gate3), lnw_ref[...], lnb_ref[...], hs_ref[...])
        delta = _dot(zz, wo_ref[...])
    x3 = take(3)
    g_ref, wg_ref, wu_ref, cw_ref, cb_ref, wd_ref, o_ref = take(7)

    t0 = pl.program_id(0) * tm
    first = (t0 % seq) == 0
    last = ((t0 + tm) % seq) == 0
    x_ext = _ext(x3)
    if delta is not None:
        x_ext = x_ext + delta
    rid = lax.broadcasted_iota(jnp.int32, (tm + 2 * HALO, 1), 0)
    outside = jnp.logical_or(jnp.logical_and(first, rid < HALO), jnp.logical_and(last, rid >= tm + HALO))
    h_ext = jnp.where(outside, 0.0, _rms_rows(x_ext, g_ref[...]))
    s = F32_SUBLANES
    ext = tm + 2 * s
    hn_ext = h_ext[HALO - s:HALO + tm + s].astype(BF16)
    hn = h_ext[HALO:HALO + tm].astype(BF16)
    x = x_ext[HALO:HALO + tm]
    cw = 0.5 * cw_ref[...]
    cb = 0.5 * cb_ref[...]

    def gate_up(c):
        cs = slice(c * FF_CHUNK, (c + 1) * FF_CHUNK)
        return _dot(hn_ext, wg_ref[:, cs]), _dot(hn, wu_ref[:, cs])

    acc = x
    nxt = gate_up(0)
    for c in range(D_FF // FF_CHUNK):
        cs = slice(c * FF_CHUNK, (c + 1) * FF_CHUNK)
        ge, up = nxt
        if c + 1 < D_FF // FF_CHUNK:
            nxt = gate_up(c + 1)
        gprev = pltpu.roll(ge, 1, axis=0)[s:s + tm]
        gnext = pltpu.roll(ge, ext - 1, axis=0)[s:s + tm]
        h = gprev * cw[0:1, cs] + ge[s:s + tm] * cw[1:2, cs] + gnext * cw[2:3, cs] + cb[:, cs]
        act = (h * (jnp.tanh(h) + 1.0) * up).astype(BF16)
        acc = acc + _dot(act, wd_ref[cs, :])
    o_ref[...] = acc


def _halo_specs(tm, n, width):
    hb = tm // HALO
    nhb = n // HALO
    return [pl.BlockSpec((HALO, width), lambda i: (jnp.maximum(i * hb - 1, 0), 0)),
            pl.BlockSpec((tm, width), lambda i: (i, 0)),
            pl.BlockSpec((HALO, width), lambda i: (jnp.minimum((i + 1) * hb, nhb - 1), 0))]


def _ffn_call(x2, g, f, seq, pre="none", pre_args=(), tm=256):
    n = x2.shape[0]
    args, specs = [], []

    def tiled(a):
        args.extend([a, a, a])
        specs.extend(_halo_specs(tm, n, a.shape[1]))

    def whole(a):
        args.append(a)
        specs.append(_const_spec(a.shape))

    if pre == "proj":
        a, w_o = pre_args
        tiled(a)
        whole(w_o)
    elif pre == "rwkv":
        yf, yb, bonus, gate, ln_w, ln_b, hs, w_o = pre_args
        for a in (yf, yb, bonus, gate):
            tiled(a)
        for a in (ln_w, ln_b, hs, w_o):
            whole(a)
    tiled(x2)
    for a in (g, f["w_gate"], f["w_up"], f["conv_w"], f["conv_b"], f["w_down"]):
        whole(a)
    return pl.pallas_call(
        functools.partial(_ffn_kernel, tm=tm, seq=seq, pre=pre),
        grid=(n // tm,),
        in_specs=specs,
        out_specs=pl.BlockSpec((tm, D_MODEL), lambda i: (i, 0)),
        out_shape=jax.ShapeDtypeStruct((n, D_MODEL), F32),
        compiler_params=_params("parallel"),
        name="ffn_" + pre,
    )(*args)


def _rwkv_in_kernel(*refs, tm, seq, has_vres):
    (x_ref, xp_ref, xn_ref, g_ref, mu_ref, wr_ref, wk_ref, wv_ref, w1_ref, w2_ref, a1_ref, a2_ref,
     g1_ref, g2_ref, w0_ref, a0_ref, kk_ref, ka_ref, rk_ref, hs_ref) = refs[:20]
    refs = refs[20:]
    if has_vres:
        vf_ref, v0_ref, v1_ref, v2_ref = refs[:4]
        refs = refs[4:]
    r_out, v_out, aa_out, g_out, bonus_out, lw_out, kd_out, bb_out = refs

    t0 = pl.program_id(0) * tm
    first = (t0 % seq) == 0
    last = ((t0 + tm) % seq) == 0
    gn = g_ref[...]
    h = _rms_rows(x_ref[...], gn)
    prev_row = jnp.where(first, 0.0, _rms_rows(xp_ref[...], gn)[F32_SUBLANES - 1:F32_SUBLANES, :])
    next_row = jnp.where(last, 0.0, _rms_rows(xn_ref[...], gn)[0:1, :])
    prev, nxt = _shift_rows(h, prev_row, next_row, tm)
    xx = 0.5 * (prev + nxt) - h
    mu = mu_ref[...]
    mixes = [(h + xx * mu[i:i + 1, :]).astype(BF16) for i in range(6)]
    hs = hs_ref[...]
    ka = ka_ref[...]
    rows = tm // RWKV_IN_SUBTILES

    def project(p):
        sl = slice(p * rows, (p + 1) * rows)
        xr, xw, xk, xv, xa, xg = (m[sl] for m in mixes)
        res = dict(sl=sl, r=_dot(xr, wr_ref[...]), k=_dot(xk, wk_ref[...]), v=_dot(xv, wv_ref[...]))
        if has_vres:
            res["vl"] = _dot(_dot(xv, v1_ref[...]).astype(BF16), v2_ref[...])
        res["g"] = _dot(_sigmoid(_dot(xg, g1_ref[...])).astype(BF16), g2_ref[...])
        res["lwl"] = _dot(jnp.tanh(_dot(xw, w1_ref[...])).astype(BF16), w2_ref[...])
        res["la"] = _dot(_dot(xa, a1_ref[...]).astype(BF16), a2_ref[...])
        return res

    def finish(res):
        sl, r, k, v = res["sl"], res["r"], res["k"], res["v"]
        if has_vres:
            v = v + (vf_ref[sl, :] - v) * _sigmoid(v0_ref[...] + res["vl"])
        g_out[sl, :] = res["g"]
        kk = k * kk_ref[...]
        kkn_parts = []
        for t in range(N_MXU_TILES):
            ks = kk[:, t * MXU_W:(t + 1) * MXU_W]
            ss = _head_sum((ks * ks).astype(BF16), hs)
            kkn_parts.append(ks * jnp.minimum(lax.rsqrt(ss), 1e12))
        kkn = jnp.concatenate(kkn_parts, axis=1)
        kd_sum = None
        for z in range(2):
            zs = slice(z * D_MODEL, (z + 1) * D_MODEL)
            lw_out[z, sl, :] = (-math.exp(-0.5)) * _sigmoid(w0_ref[:, zs] + res["lwl"][:, zs])
            a = _sigmoid(a0_ref[:, zs] + res["la"][:, zs])
            kd = k * (1.0 + (a - 1.0) * ka)
            kd_out[z, sl, :] = kd
            bb_out[z, sl, :] = kkn * a
            kd_sum = kd if kd_sum is None else kd_sum + kd
        rkk = r * kd_sum * rk_ref[...]
        for t in range(N_MXU_TILES):
            cs = slice(t * MXU_W, (t + 1) * MXU_W)
            hi, lo = _split2(rkk[:, cs])
            bonus_out[sl, cs] = (_head_sum(hi, hs) + _head_sum(lo, hs)) * v[:, cs]
        r_out[sl, :] = r
        v_out[sl, :] = v
        aa_out[sl, :] = -kkn

    for res in [project(p) for p in range(RWKV_IN_SUBTILES)]:
        finish(res)


def _rwkv_in_call(x2, seq, p, v_first, tm=256):
    n = x2.shape[0]
    has_vres = v_first is not None
    hb = tm // F32_SUBLANES
    nhb = n // F32_SUBLANES
    tok = pl.BlockSpec((tm, D_MODEL), lambda i: (i, 0))
    tok2 = pl.BlockSpec((2, tm, D_MODEL), lambda i: (0, i, 0))
    halo_p = pl.BlockSpec((F32_SUBLANES, D_MODEL), lambda i: (jnp.maximum(i * hb - 1, 0), 0))
    halo_n = pl.BlockSpec((F32_SUBLANES, D_MODEL), lambda i: (jnp.minimum((i + 1) * hb, nhb - 1), 0))
    args = [x2, x2, x2, p["g"], p["mu"], p["w_r"], p["w_k"], p["w_v"], p["w1"], p["w2"], p["a1"], p["a2"],
            p["g1"], p["g2"], p["w0"], p["a0"], p["k_k"], p["k_a"], p["r_k"], p["hs"]]
    specs = [tok, halo_p, halo_n] + [_const_spec(a.shape) for a in args[3:]]
    if has_vres:
        extra = [v_first, p["v0"], p["v1"], p["v2"]]
        args += extra
        specs += [tok] + [_const_spec(a.shape) for a in extra[1:]]
    o1 = jax.ShapeDtypeStruct((n, D_MODEL), F32)
    o2 = jax.ShapeDtypeStruct((2, n, D_MODEL), F32)
    return pl.pallas_call(
        functools.partial(_rwkv_in_kernel, tm=tm, seq=seq, has_vres=has_vres),
        grid=(n // tm,),
        in_specs=specs,
        out_specs=[tok] * 5 + [tok2] * 3,
        out_shape=[o1] * 5 + [o2] * 3,
        compiler_params=_params("parallel"),
        name="rwkv_in",
    )(*args)


def _wkv_kernel(rf_ref, vf_ref, af_ref, rb_ref, vb_ref, ab_ref, lwf_ref, kf_ref, bf_ref, lwb_ref, kb_ref, bb_ref,
                yf_ref, yb_ref, s_ref, *, tb, nt):
    assert CHUNK == HEAD_DIM
    nchunk = tb // CHUNK
    c1 = CHUNK

    @pl.when(pl.program_id(2) == 0)
    def _():
        s_ref[...] = jnp.zeros_like(s_ref)

    lane = lax.broadcasted_iota(jnp.int32, (c1, LANES), 1)
    d = lax.broadcasted_iota(jnp.int32, (c1, LANES), 0) - (lane & (c1 - 1))
    rid = lax.broadcasted_iota(jnp.int32, (c1, 1), 0)
    eye = (d == 0).astype(F32)
    m0 = lane < HEAD_DIM
    m0w = jnp.concatenate([m0, m0], axis=1)
    dirs = (
        (rf_ref, vf_ref, af_ref, lwf_ref, kf_ref, bf_ref, yf_ref, d > 0, d >= 0, False),
        (rb_ref, vb_ref, ab_ref, lwb_ref, kb_ref, bb_ref, yb_ref, d < 0, d <= 0, True),
    )

    def running_sum(x, backward):
        sh = 1
        while sh < c1:
            if backward:
                x = x + jnp.where(rid < c1 - sh, pltpu.roll(x, c1 - sh, axis=0), 0.0)
            else:
                x = x + jnp.where(rid >= sh, pltpu.roll(x, sh, axis=0), 0.0)
            sh *= 2
        return x

    def stack(x, mask=m0):
        zero = jnp.zeros_like(x)
        return jnp.concatenate([jnp.where(mask, x, zero), jnp.where(mask, zero, x)], axis=0)

    def chunk_body(ci, carry):
        st = []
        for z, (r_ref, v_ref, a_ref, lw_ref, k_ref, b_ref, y_ref, strict, incl, backward) in enumerate(dirs):
            c = ci if z == 0 else nchunk - 1 - ci
            sl = pl.ds(pl.multiple_of(c * CHUNK, CHUNK), CHUNK)
            lw_all = lw_ref[0, 0, sl, :]
            cl_all = running_sum(lw_all, backward)
            for t in range(nt):
                ls = slice(t * LANES, (t + 1) * LANES)
                lw = lw_all[:, ls]
                cl = cl_all[:, ls]
                e_in = jnp.exp(cl)
                e_ex = jnp.exp(cl - lw)
                e_neg = jnp.exp(-cl)
                q = dict(zi=z, t=t, sl=sl, ls=ls, y_ref=y_ref, strict=strict, incl=incl)
                tot = jnp.sum(lw, axis=0, keepdims=True)
                e_out = jnp.exp(tot - cl)
                bb, kd = b_ref[0, 0, sl, ls], k_ref[0, 0, sl, ls]
                q["wcol"] = jnp.broadcast_to(jnp.exp(tot), (HEADS_PER_TILE * c1, LANES)).T
                q["ar"] = jnp.concatenate([a_ref[0, sl, ls] * e_ex, r_ref[0, sl, ls] * e_in], axis=0).astype(BF16)
                q["bk"] = jnp.concatenate([stack((bb * e_neg).astype(BF16)), stack((kd * e_neg).astype(BF16))], axis=0)
                q["bk_t"] = jnp.concatenate([stack(bb * e_out).T, stack(kd * e_out).T], axis=1).astype(BF16)
                q["vs"] = stack(v_ref[0, sl, ls].astype(BF16))
                st.append(q)
        for q in st:
            q["g"] = _dot_nt(q["ar"], q["bk"])
        for q in st:
            g = q.pop("g")
            a_ab = jnp.where(q["strict"], g[:c1, :LANES], 0.0)
            q["a_ak"] = jnp.where(q["strict"], g[:c1, LANES:], 0.0).astype(BF16)
            q["a_r"] = jnp.concatenate([jnp.where(q["incl"], g[c1:, :LANES], 0.0),
                                        jnp.where(q["incl"], g[c1:, LANES:], 0.0)], axis=1).astype(BF16)
            q["tm"] = eye + a_ab
            q["pw"] = a_ab.astype(BF16)
        levels = CHUNK.bit_length() - 1
        for q in st:
            q["pw"] = _dot(q["pw"], stack(q["pw"])).astype(BF16)
        for _ in range(levels - 2):
            for q in st:
                pt = jnp.concatenate([q["pw"], q["tm"].astype(BF16)], axis=0)
                q["z"] = _dot(pt, stack(q["pw"]))
            for q in st:
                z = q.pop("z")
                q["pw"] = z[:c1].astype(BF16)
                q["tm"] = q["tm"] + z[c1:]
        for q in st:
            q["tm"] = q["tm"] + _dot(q["pw"], stack(q["tm"].astype(BF16)))
        for q in st:
            q["s"] = s_ref[q["zi"], q["t"]]
            xy = _dot(q["ar"], q["s"].astype(BF16))
            q["y_s"] = xy[c1:]
            q["x"] = xy[:c1] + _dot(q["a_ak"], q["vs"])
        for q in st:
            q["u"] = stack(_dot(q["tm"].astype(BF16), stack(q["x"].astype(BF16))).astype(BF16))
        for q in st:
            uv = jnp.concatenate([q["u"], q["vs"]], axis=0)
            z = _dot(jnp.concatenate([q["a_r"], q["bk_t"]], axis=0), uv)
            q["y_ref"][0, q["sl"], q["ls"]] = q["y_s"] + z[:c1]
            s_ref[q["zi"], q["t"]] = q["s"] * q["wcol"] + z[c1:]
        return carry

    lax.fori_loop(0, nchunk, chunk_body, 0, unroll=2)


def _wkv_call(r, v, aa, lw, kd, bb, tb=256, nt=8):
    b, t, _ = r.shape
    tb = min(tb, t)
    ntb = t // tb
    w = nt * LANES
    fwd = pl.BlockSpec((1, tb, w), lambda bi, hi, j: (bi, j, hi))
    bwd = pl.BlockSpec((1, tb, w), lambda bi, hi, j: (bi, ntb - 1 - j, hi))
    fwd_d = pl.BlockSpec((1, 1, tb, w), lambda bi, hi, j: (0, bi, j, hi))
    bwd_d = pl.BlockSpec((1, 1, tb, w), lambda bi, hi, j: (1, bi, ntb - 1 - j, hi))
    out = jax.ShapeDtypeStruct((b, t, D_MODEL), F32)
    return pl.pallas_call(
        functools.partial(_wkv_kernel, tb=tb, nt=nt),
        grid=(b, N_HEAD_TILES // nt, ntb),
        in_specs=[fwd, fwd, fwd, bwd, bwd, bwd, fwd_d, fwd_d, fwd_d, bwd_d, bwd_d, bwd_d],
        out_specs=[fwd, bwd],
        out_shape=[out, out],
        scratch_shapes=[pltpu.VMEM((2, nt, LANES, LANES), F32)],
        compiler_params=_params("parallel", "parallel", "arbitrary"),
        name="wkv",
    )(r, v, aa, r, v, aa, lw, kd, bb, lw, kd, bb)


def _rwkv_gate(y, bonus, g, lnw, lnb, hs):
    parts = []
    for t in range(N_MXU_TILES):
        ys = y[:, t * MXU_W:(t + 1) * MXU_W]
        hi, lo = _split2(ys)
        mean = (_head_sum(hi, hs) + _head_sum(lo, hs)) * (1.0 / HEAD_DIM)
        dlt = ys - mean
        var = _head_sum((dlt * dlt).astype(BF16), hs) * (1.0 / HEAD_DIM)
        parts.append(dlt * lax.rsqrt(var + GN_EPS))
    yn = jnp.concatenate(parts, axis=1) * lnw + lnb
    return ((yn + bonus) * g).astype(BF16)


def _row(v):
    return v.reshape(1, -1).astype(F32)


def _pad_to(a, axis, size):
    pad = [(0, 0)] * a.ndim
    pad[axis] = (0, size - a.shape[axis])
    return jnp.pad(a, pad)


def _block_diag2(m):
    l, d = m.shape[1], m.shape[2]
    zero = jnp.zeros((l, d), m.dtype)
    return jnp.concatenate([jnp.concatenate([m[0], zero], axis=1), jnp.concatenate([zero, m[1]], axis=1)], axis=0)


def _round_up(n, m):
    return (n + m - 1) // m * m


def _prepare(w):
    hs = np.kron(np.eye(MXU_W // HEAD_DIM), np.ones((HEAD_DIM, HEAD_DIM)))
    hs = jnp.asarray(hs, BF16)
    na = []
    for li in range(w["na_w_qkv"].shape[0]):
        na.append(dict(
            w_qkv=w["na_w_qkv"][li].astype(BF16),
            w_o=w["na_w_o"][li].astype(BF16),
            gq=_row(jnp.tile(w["na_q_gain"][li], N_HEADS)) * (HEAD_DIM ** -0.5),
            gk=_row(jnp.tile(w["na_k_gain"][li], N_HEADS)),
            tbl=_na_bias_table(w["na_rpb"][li]),
        ))
    rw = []
    for li in range(w["rw_w_r"].shape[0]):
        lg = _round_up(w["rw_g1"].shape[-1], LANES)
        p = dict(
            mu=w["rw_mu"][li].astype(F32),
            w_r=w["rw_w_r"][li].astype(BF16), w_k=w["rw_w_k"][li].astype(BF16),
            w_v=w["rw_w_v"][li].astype(BF16), w_o=w["rw_w_o"][li].astype(BF16),
            w1=jnp.concatenate([w["rw_w1"][li, 0], w["rw_w1"][li, 1]], axis=1).astype(BF16),
            w2=_block_diag2(w["rw_w2"][li]).astype(BF16),
            a1=jnp.concatenate([w["rw_a1"][li, 0], w["rw_a1"][li, 1]], axis=1).astype(BF16),
            a2=_block_diag2(w["rw_a2"][li]).astype(BF16),
            g1=_pad_to(w["rw_g1"][li], 1, lg).astype(BF16),
            g2=_pad_to(w["rw_g2"][li], 0, lg).astype(BF16),
            w0=_row(w["rw_w0"][li]), a0=_row(w["rw_a0"][li]),
            k_k=_row(w["rw_k_k"][li]), k_a=_row(w["rw_k_a"][li]), r_k=_row(w["rw_r_k"][li]),
            ln_w=_row(w["rw_ln_w"][li]), ln_b=_row(w["rw_ln_b"][li]),
            hs=hs,
        )
        if li >= 1:
            lv = _round_up(w["rw_v1"].shape[-1], LANES)
            p.update(v0=_row(w["rw_v0"][li - 1]),
                     v1=_pad_to(w["rw_v1"][li - 1], 1, lv).astype(BF16),
                     v2=_pad_to(w["rw_v2"][li - 1], 0, lv).astype(BF16))
        rw.append(p)
    ffn = []
    for i in range(DEPTH):
        ffn.append(dict(
            w_gate=w["ffn_w_gate"][i].astype(BF16), w_up=w["ffn_w_up"][i].astype(BF16),
            conv_w=w["ffn_conv_w"][i].astype(F32), conv_b=_row(w["ffn_conv_b"][i]),
            w_down=w["ffn_w_down"][i].astype(BF16),
        ))
    return dict(na=na, rw=rw, ffn=ffn, hs=hs,
                norm_mix=w["norm_mix"].astype(F32), norm_ffn=w["norm_ffn"].astype(F32))


def _trunk(x, p):
    b, t, d = x.shape
    n = b * t
    x2 = x.reshape(n, d)
    v_first = None
    for i in range(DEPTH):
        li = i // 2
        g_mix = _row(p["norm_mix"][i])
        if i % 2 == 0:
            a = p["na"][li]
            q, k, v = _qkv_call(x2, g_mix, a["w_qkv"], a["gq"], a["gk"], p["hs"])
            o = _na_call(q.reshape(b, t, d), k.reshape(b, t, d), v.reshape(b, t, d), a["tbl"])
            pre, pre_args = "proj", (o.reshape(n, d), a["w_o"])
        else:
            rp = dict(p["rw"][li], g=g_mix)
            r, v, aa, g, bonus, lw, kd, bb = _rwkv_in_call(x2, t, rp, v_first)
            if v_first is None:
                v_first = v
            yf, yb = _wkv_call(r.reshape(b, t, d), v.reshape(b, t, d), aa.reshape(b, t, d),
                               lw.reshape(2, b, t, d), kd.reshape(2, b, t, d), bb.reshape(2, b, t, d))
            pre, pre_args = "rwkv", (yf.reshape(n, d), yb.reshape(n, d), bonus, g, rp["ln_w"], rp["ln_b"], p["hs"],
                                     rp["w_o"])
        x2 = _ffn_call(x2, _row(p["norm_ffn"][i]), p["ffn"][i], t, pre, pre_args)
    return x2.reshape(b, t, d)


def kernel(x_prompt, x_sample, norm_mix, norm_ffn, na_w_qkv, na_w_o, na_q_gain, na_k_gain, na_rpb, rw_mu, rw_w_r, rw_w_k, rw_w_v, rw_w_o, rw_w0, rw_w1, rw_w2, rw_a0, rw_a1, rw_a2, rw_v0, rw_v1, rw_v2, rw_g1, rw_g2, rw_k_k, rw_k_a, rw_r_k, rw_ln_w, rw_ln_b, ffn_w_gate, ffn_w_up, ffn_conv_w, ffn_conv_b, ffn_w_down):
    w = dict(norm_mix=norm_mix, norm_ffn=norm_ffn, na_w_qkv=na_w_qkv, na_w_o=na_w_o, na_q_gain=na_q_gain,
             na_k_gain=na_k_gain, na_rpb=na_rpb, rw_mu=rw_mu, rw_w_r=rw_w_r, rw_w_k=rw_w_k, rw_w_v=rw_w_v,
             rw_w_o=rw_w_o, rw_w0=rw_w0, rw_w1=rw_w1, rw_w2=rw_w2, rw_a0=rw_a0, rw_a1=rw_a1, rw_a2=rw_a2,
             rw_v0=rw_v0, rw_v1=rw_v1, rw_v2=rw_v2, rw_g1=rw_g1, rw_g2=rw_g2, rw_k_k=rw_k_k, rw_k_a=rw_k_a,
             rw_r_k=rw_r_k, rw_ln_w=rw_ln_w, rw_ln_b=rw_ln_b, ffn_w_gate=ffn_w_gate, ffn_w_up=ffn_w_up,
             ffn_conv_w=ffn_conv_w, ffn_conv_b=ffn_conv_b, ffn_w_down=ffn_w_down)
    p = _prepare(w)
    return (_trunk(x_prompt, p), _trunk(x_sample, p))
```

```python
import functools
import math

import numpy as np
import jax
import jax.numpy as jnp
from jax import lax
from jax.experimental import pallas as pl
from jax.experimental.pallas import tpu as pltpu

F32 = jnp.float32
BF16 = jnp.bfloat16

D_MODEL = 1024
GRID_W = 64
N_HEADS = 16
HEAD_DIM = 64
WIN_H = 8
WIN_W = 16
RPB_H = 2 * WIN_H - 1
RPB_W = 2 * WIN_W - 1
D_FF = 2816
DEPTH = 4
RMS_EPS = 1e-6
GN_EPS = 64e-5
NEG_INF = -1e30

LANES = 128
HEADS_PER_TILE = LANES // HEAD_DIM
N_HEAD_TILES = D_MODEL // LANES
MXU_W = 256
N_MXU_TILES = D_MODEL // MXU_W
F32_SUBLANES = 8
HALO = 16
FF_CHUNK = MXU_W
CHUNK = 64
NA_ROWS_PER_STEP = 16
VMEM_LIMIT = 56 * 1024 * 1024


def _params(*sem):
    return pltpu.CompilerParams(dimension_semantics=sem, vmem_limit_bytes=VMEM_LIMIT)


def _const_spec(shape):
    nd = len(shape)
    return pl.BlockSpec(shape, lambda *_: (0,) * nd, pipeline_mode=pl.Buffered(1))


def _dot(a, b):
    return jnp.dot(a, b, preferred_element_type=F32)


def _dot_nt(a, b):
    return lax.dot_general(a, b, (((1,), (1,)), ((), ())), preferred_element_type=F32)


def _rms_rows(xf, g):
    ms = jnp.mean(xf * xf, axis=-1, keepdims=True)
    return xf * lax.rsqrt(ms + RMS_EPS) * g


def _head_sum(x_bf16, hsum):
    return _dot(x_bf16, hsum)


def _sigmoid(x):
    return 0.5 * jnp.tanh(0.5 * x) + 0.5


def _split2(x):
    hi = x.astype(BF16)
    lo = (x - hi.astype(F32)).astype(BF16)
    return hi, lo


def _split3(x):
    hi = x.astype(BF16)
    r1 = x - hi.astype(F32)
    mid = r1.astype(BF16)
    lo = (r1 - mid.astype(F32)).astype(BF16)
    return hi, mid, lo


def _qkv_kernel(x_ref, g_ref, w_ref, gq_ref, gk_ref, hs_ref, q_ref, k_ref, v_ref):
    hn = _rms_rows(x_ref[...], g_ref[...]).astype(BF16)
    hs = hs_ref[...]
    for part, (o_ref, gain_ref) in enumerate(((q_ref, gq_ref), (k_ref, gk_ref))):
        y = _dot(hn, w_ref[:, part * D_MODEL:(part + 1) * D_MODEL])
        gain = gain_ref[...]
        for t in range(N_MXU_TILES):
            sl = slice(t * MXU_W, (t + 1) * MXU_W)
            ys = y[:, sl]
            ms = _head_sum((ys * ys).astype(BF16), hs) * (1.0 / HEAD_DIM)
            o_ref[:, sl] = (ys * lax.rsqrt(ms + RMS_EPS) * gain[:, sl]).astype(BF16)
    v_ref[...] = _dot(hn, w_ref[:, 2 * D_MODEL:]).astype(BF16)


def _qkv_call(x2, g, w_qkv, gq, gk, hs, tm=512):
    n = x2.shape[0]
    tok = pl.BlockSpec((tm, D_MODEL), lambda i: (i, 0))
    out = jax.ShapeDtypeStruct((n, D_MODEL), BF16)
    return pl.pallas_call(
        _qkv_kernel,
        grid=(n // tm,),
        in_specs=[tok, _const_spec((1, D_MODEL)), _const_spec((D_MODEL, 3 * D_MODEL)),
                  _const_spec((1, D_MODEL)), _const_spec((1, D_MODEL)), _const_spec((MXU_W, MXU_W))],
        out_specs=[tok, tok, tok],
        out_shape=[out, out, out],
        compiler_params=_params("parallel"),
        name="na_qkv",
    )(x2, g, w_qkv, gq, gk, hs)


def _na_kernel(q_ref, k_ref, v_ref, tbl_ref, o_ref, *, rows):
    kh = min(WIN_H, rows)
    lane = lax.broadcasted_iota(jnp.int32, (GRID_W, LANES), 1)
    head_mask = [lane < HEAD_DIM, lane >= HEAD_DIM]

    def rows_body(rb, carry):
        st = []
        for ri in range(NA_ROWS_PER_STEP):
            r = rb * NA_ROWS_PER_STEP + ri
            rs = jnp.clip(r - kh // 2, 0, rows - kh)
            qsl = pl.ds(pl.multiple_of(r * GRID_W, GRID_W), GRID_W)
            ksl = pl.ds(pl.multiple_of(rs * GRID_W, GRID_W), kh * GRID_W)
            q = q_ref[0, qsl, :]
            zero = jnp.zeros_like(q)
            qs = jnp.concatenate([jnp.where(head_mask[0], q, zero), jnp.where(head_mask[1], q, zero)], axis=0)
            st.append(dict(dr0=rs - r + WIN_H - 1, ksl=ksl, qsl=qsl, s=_dot_nt(qs, k_ref[0, ksl, :])))
        for e in st:
            bias = jnp.concatenate([tbl_ref[0, e["dr0"] + 2 * p] for p in range(kh // 2)], axis=1)
            s = e.pop("s") + bias
            m = jnp.max(s, axis=1, keepdims=True)
            p = jnp.exp(s - m)
            e["l"] = jnp.sum(p, axis=1, keepdims=True)
            e["p"] = p.astype(BF16)
        for e in st:
            e["o"] = _dot(e.pop("p"), v_ref[0, e["ksl"], :]) / e["l"]
        for e in st:
            o = e["o"]
            o_ref[0, e["qsl"], :] = jnp.where(head_mask[0], o[:GRID_W], o[GRID_W:]).astype(BF16)
        return carry

    lax.fori_loop(0, rows // NA_ROWS_PER_STEP, rows_body, 0)


def _na_call(q, k, v, tbl):
    b, t, _ = q.shape
    rows = t // GRID_W
    assert rows >= WIN_H and WIN_H % 2 == 0
    blk = pl.BlockSpec((1, t, LANES), lambda bi, hi: (bi, 0, hi))
    return pl.pallas_call(
        functools.partial(_na_kernel, rows=rows),
        grid=(b, N_HEAD_TILES),
        in_specs=[blk, blk, blk,
                  pl.BlockSpec((1, RPB_H - 1, HEADS_PER_TILE * GRID_W, LANES), lambda bi, hi: (hi, 0, 0, 0))],
        out_specs=blk,
        out_shape=jax.ShapeDtypeStruct((b, t, D_MODEL), BF16),
        compiler_params=_params("parallel", "parallel"),
        name="na_attn",
    )(q, k, v, tbl)


def _na_bias_table(rpb):
    c = np.arange(GRID_W)[:, None]
    kc = np.arange(GRID_W)[None, :]
    cs = np.clip(c - WIN_W // 2, 0, GRID_W - WIN_W)
    ok = (kc >= cs) & (kc < cs + WIN_W)
    d_col = np.clip(kc - c + WIN_W - 1, 0, RPB_W - 1)
    base = jnp.where(ok[None, None], rpb.astype(F32)[:, :, d_col], NEG_INF)
    pairs = jnp.concatenate([base[:, :-1], base[:, 1:]], axis=-1)
    pairs = pairs.reshape(N_HEAD_TILES, HEADS_PER_TILE, RPB_H - 1, GRID_W, LANES)
    return jnp.transpose(pairs, (0, 2, 1, 3, 4)).reshape(N_HEAD_TILES, RPB_H - 1, HEADS_PER_TILE * GRID_W, LANES)


def _shift_rows(cur, prev_row, next_row, tm):
    rid = lax.broadcasted_iota(jnp.int32, (F32_SUBLANES, 1), 0)
    down = pltpu.roll(cur, 1, axis=0)
    up = pltpu.roll(cur, tm - 1, axis=0)
    prev = jnp.concatenate([jnp.where(rid == 0, prev_row, down[:F32_SUBLANES]), down[F32_SUBLANES:]], axis=0)
    nxt = jnp.concatenate([up[:tm - F32_SUBLANES],
                           jnp.where(rid == F32_SUBLANES - 1, next_row, up[tm - F32_SUBLANES:])], axis=0)
    return prev, nxt


def _ext(refs3):
    return jnp.concatenate([r[...] for r in refs3], axis=0)


def _pre_phases(pre, blocks, consts):
    st = {}
    if pre == "proj":
        a3, x3 = blocks
        (wo_ref,) = consts

        def last():
            return _ext(x3) + _dot(_ext(a3), wo_ref[...])

        return [lambda: None, lambda: None, last]

    yf3, yb3, bonus3, gate3, x3 = blocks
    lnw_ref, lnb_ref, hs_ref, wo_ref = consts
    tiles = [slice(t * MXU_W, (t + 1) * MXU_W) for t in range(N_MXU_TILES)]

    def mean():
        st["y"] = _ext(yf3) + _ext(yb3)
        sums = []
        for sl in tiles:
            hi, lo = _split2(st["y"][:, sl])
            sums.append(_head_sum(hi, hs_ref[...]) + _head_sum(lo, hs_ref[...]))
        st["sum"] = sums

    def var():
        st["dlt"] = [st["y"][:, sl] - sm * (1.0 / HEAD_DIM) for sl, sm in zip(tiles, st["sum"])]
        st["sq"] = [_head_sum((d * d).astype(BF16), hs_ref[...]) for d in st["dlt"]]

    def last():
        yn = jnp.concatenate([d * lax.rsqrt(sq * (1.0 / HEAD_DIM) + GN_EPS) for d, sq in zip(st["dlt"], st["sq"])],
                             axis=1)
        yn = yn * lnw_ref[...] + lnb_ref[...]
        zz = ((yn + _ext(bonus3).astype(F32)) * _ext(gate3).astype(F32)).astype(BF16)
        return _ext(x3) + _dot(zz, wo_ref[...])

    return [mean, var, last]


PRE_AFTER_CHUNK = (1, 3, 6)


def _ffn_kernel(*refs, tm, seq, pre):
    n_tiled, n_const = (2, 1) if pre == "proj" else (5, 4)
    it = iter(refs)

    def take(k):
        return [next(it) for _ in range(k)]

    first_blocks = [take(3) for _ in range(n_tiled)]
    next_blocks = [take(3) for _ in range(n_tiled)]
    consts = take(n_const)
    g_ref, wg_ref, wu_ref, cw_ref, cb_ref, wd_ref, o_ref, xs_ref = take(8)

    @pl.when(pl.program_id(0) == 0)
    def _():
        phases = _pre_phases(pre, first_blocks, consts)
        phases[0]()
        phases[1]()
        xs_ref[...] = phases[2]()

    t0 = pl.program_id(0) * tm
    first = (t0 % seq) == 0
    last = ((t0 + tm) % seq) == 0
    x_ext = xs_ref[...]
    rid = lax.broadcasted_iota(jnp.int32, (tm + 2 * HALO, 1), 0)
    outside = jnp.logical_or(jnp.logical_and(first, rid < HALO), jnp.logical_and(last, rid >= tm + HALO))
    h_ext = jnp.where(outside, 0.0, _rms_rows(x_ext, g_ref[...]))
    s = F32_SUBLANES
    ext = tm + 2 * s
    hn_ext = h_ext[HALO - s:HALO + tm + s].astype(BF16)
    hn = h_ext[HALO:HALO + tm].astype(BF16)
    cw = 0.5 * cw_ref[...]
    cb = 0.5 * cb_ref[...]

    def gate_up(c):
        cs = slice(c * FF_CHUNK, (c + 1) * FF_CHUNK)
        return _dot(hn_ext, wg_ref[:, cs]), _dot(hn, wu_ref[:, cs])

    next_phases = _pre_phases(pre, next_blocks, consts)
    x_next = None
    acc = x_ext[HALO:HALO + tm]
    nxt = gate_up(0)
    for c in range(D_FF // FF_CHUNK):
        cs = slice(c * FF_CHUNK, (c + 1) * FF_CHUNK)
        ge, up = nxt
        if c + 1 < D_FF // FF_CHUNK:
            nxt = gate_up(c + 1)
        gprev = pltpu.roll(ge, 1, axis=0)[s:s + tm]
        gnext = pltpu.roll(ge, ext - 1, axis=0)[s:s + tm]
        h = gprev * cw[0:1, cs] + ge[s:s + tm] * cw[1:2, cs] + gnext * cw[2:3, cs] + cb[:, cs]
        act = (h * (jnp.tanh(h) + 1.0) * up).astype(BF16)
        acc = acc + _dot(act, wd_ref[cs, :])
        if c in PRE_AFTER_CHUNK:
            x_next = next_phases[PRE_AFTER_CHUNK.index(c)]()
    o_ref[...] = acc
    xs_ref[...] = x_next


def _halo_specs(tm, n, width, tile_of_step, single=False):
    hb = tm // HALO
    nhb = n // HALO
    mode = dict(pipeline_mode=pl.Buffered(1)) if single else {}
    return [pl.BlockSpec((HALO, width), lambda i: (jnp.maximum(tile_of_step(i) * hb - 1, 0), 0), **mode),
            pl.BlockSpec((tm, width), lambda i: (tile_of_step(i), 0), **mode),
            pl.BlockSpec((HALO, width), lambda i: (jnp.minimum((tile_of_step(i) + 1) * hb, nhb - 1), 0), **mode)]


def _ffn_call(x2, g, f, seq, pre, pre_args, tm=256):
    n = x2.shape[0]
    ntiles = n // tm
    if pre == "proj":
        a, w_o = pre_args
        tiled, consts = [a, x2], [w_o]
    else:
        yf, yb, bonus, gate, ln_w, ln_b, hs, w_o = pre_args
        tiled, consts = [yf, yb, bonus, gate, x2], [ln_w, ln_b, hs, w_o]
    consts = consts + [g, f["w_gate"], f["w_up"], f["conv_w"], f["conv_b"], f["w_down"]]
    args, specs = [], []
    for tile_of_step, single in ((lambda i: i * 0, True), (lambda i: jnp.minimum(i + 1, ntiles - 1), False)):
        for a in tiled:
            args.extend([a, a, a])
            specs.extend(_halo_specs(tm, n, a.shape[1], tile_of_step, single))
    for a in consts:
        args.append(a)
        specs.append(_const_spec(a.shape))
    return pl.pallas_call(
        functools.partial(_ffn_kernel, tm=tm, seq=seq, pre=pre),
        grid=(ntiles,),
        in_specs=specs,
        out_specs=pl.BlockSpec((tm, D_MODEL), lambda i: (i, 0)),
        out_shape=jax.ShapeDtypeStruct((n, D_MODEL), F32),
        scratch_shapes=[pltpu.VMEM((tm + 2 * HALO, D_MODEL), F32)],
        compiler_params=_params("arbitrary"),
        name="ffn_" + pre,
    )(*args)


def _rwkv_in_kernel(*refs, tm, seq, has_vres):
    (x_ref, xp_ref, xn_ref, g_ref, mu_ref, wr_ref, wk_ref, wv_ref, w1_ref, w2_ref, a1_ref, a2_ref,
     g1_ref, g2_ref, w0_ref, a0_ref, kk_ref, ka_ref, rk_ref, hs_ref) = refs[:20]
    refs = refs[20:]
    if has_vres:
        vf_ref, v0_ref, v1_ref, v2_ref = refs[:4]
        refs = refs[4:]
    r_out, v_out, aa_out, g_out, bonus_out, lw_out, kd_out, bb_out = refs

    t0 = pl.program_id(0) * tm
    first = (t0 % seq) == 0
    last = ((t0 + tm) % seq) == 0
    gn = g_ref[...]
    h = _rms_rows(x_ref[...], gn)
    prev_row = jnp.where(first, 0.0, _rms_rows(xp_ref[...], gn)[F32_SUBLANES - 1:F32_SUBLANES, :])
    next_row = jnp.where(last, 0.0, _rms_rows(xn_ref[...], gn)[0:1, :])
    prev, nxt = _shift_rows(h, prev_row, next_row, tm)
    xx = 0.5 * (prev + nxt) - h
    mu = mu_ref[...]

    def mix(i):
        return (h + xx * mu[i:i + 1, :]).astype(BF16)

    hs = hs_ref[...]
    tiles = [slice(t * MXU_W, (t + 1) * MXU_W) for t in range(N_MXU_TILES)]

    la = _dot(_dot(mix(4), a1_ref[...]).astype(BF16), a2_ref[...])
    lwl = _dot(jnp.tanh(_dot(mix(1), w1_ref[...])).astype(BF16), w2_ref[...])
    k = _dot(mix(2), wk_ref[...])
    kk = k * kk_ref[...]
    kkn = jnp.concatenate(
        [kk[:, sl] * jnp.minimum(lax.rsqrt(_head_sum((kk[:, sl] * kk[:, sl]).astype(BF16), hs)), 1e12)
         for sl in tiles], axis=1)
    aa_out[...] = (-kkn).astype(BF16)
    ka1, ka2 = ka_ref[0:1, :], ka_ref[1:2, :]
    half_decay = -0.5 * math.exp(-0.5)
    kd_sum = None
    for z in range(2):
        zs = slice(z * D_MODEL, (z + 1) * D_MODEL)
        lw_out[z] = half_decay * jnp.tanh(w0_ref[:, zs] + lwl[:, zs]) + half_decay
        th = jnp.tanh(a0_ref[:, zs] + la[:, zs])
        a = 0.5 * th + 0.5
        kd = k * (ka1 + ka2 * th)
        kd_out[z] = kd.astype(BF16)
        bb_out[z] = (kkn * a).astype(BF16)
        kd_sum = kd if kd_sum is None else kd_sum + kd

    r = _dot(mix(0), wr_ref[...])
    r_out[...] = r.astype(BF16)
    rkk = r * kd_sum * rk_ref[...]
    rk_heads = []
    for sl in tiles:
        hi, lo = _split2(rkk[:, sl])
        rk_heads.append(_head_sum(hi, hs) + _head_sum(lo, hs))
    xv = mix(3)
    v = _dot(xv, wv_ref[...])
    if has_vres:
        vl = _dot(_dot(xv, v1_ref[...]).astype(BF16), v2_ref[...])
        v = v + (vf_ref[...] - v) * _sigmoid(v0_ref[...] + vl)
    v_out[...] = v
    bonus_out[...] = (jnp.concatenate(rk_heads, axis=1) * v).astype(BF16)
    g_out[...] = _dot(_sigmoid(_dot(mix(5), g1_ref[...])).astype(BF16), g2_ref[...]).astype(BF16)


def _rwkv_in_call(x2, seq, p, v_first, tm=256):
    n = x2.shape[0]
    has_vres = v_first is not None
    hb = tm // F32_SUBLANES
    nhb = n // F32_SUBLANES
    tok = pl.BlockSpec((tm, D_MODEL), lambda i: (i, 0))
    tok2 = pl.BlockSpec((2, tm, D_MODEL), lambda i: (0, i, 0))
    halo_p = pl.BlockSpec((F32_SUBLANES, D_MODEL), lambda i: (jnp.maximum(i * hb - 1, 0), 0))
    halo_n = pl.BlockSpec((F32_SUBLANES, D_MODEL), lambda i: (jnp.minimum((i + 1) * hb, nhb - 1), 0))
    args = [x2, x2, x2, p["g"], p["mu"], p["w_r"], p["w_k"], p["w_v"], p["w1"], p["w2"], p["a1"], p["a2"],
            p["g1"], p["g2"], p["w0"], p["a0"], p["k_k"], p["k_a"], p["r_k"], p["hs"]]
    specs = [tok, halo_p, halo_n] + [_const_spec(a.shape) for a in args[3:]]
    if has_vres:
        extra = [v_first, p["v0"], p["v1"], p["v2"]]
        args += extra
        specs += [tok] + [_const_spec(a.shape) for a in extra[1:]]
    def one(dt):
        return jax.ShapeDtypeStruct((n, D_MODEL), dt)

    def two(dt):
        return jax.ShapeDtypeStruct((2, n, D_MODEL), dt)

    return pl.pallas_call(
        functools.partial(_rwkv_in_kernel, tm=tm, seq=seq, has_vres=has_vres),
        grid=(n // tm,),
        in_specs=specs,
        out_specs=[tok] * 5 + [tok2] * 3,
        out_shape=[one(BF16), one(F32), one(BF16), one(BF16), one(BF16), two(F32), two(BF16), two(BF16)],
        compiler_params=_params("parallel"),
        name="rwkv_in",
    )(*args)


def _wkv_kernel(rf_ref, vf_ref, af_ref, rb_ref, vb_ref, ab_ref, lwf_ref, kf_ref, bf_ref, lwb_ref, kb_ref, bb_ref,
                yf_ref, yb_ref, s_ref, *, tb, nt):
    assert CHUNK == HEAD_DIM
    nchunk = tb // CHUNK
    c1 = CHUNK

    @pl.when(pl.program_id(2) == 0)
    def _():
        s_ref[...] = jnp.zeros_like(s_ref)

    lane = lax.broadcasted_iota(jnp.int32, (c1, LANES), 1)
    d = lax.broadcasted_iota(jnp.int32, (c1, LANES), 0) - (lane & (c1 - 1))
    rid = lax.broadcasted_iota(jnp.int32, (c1, 1), 0)
    eye = (d == 0).astype(F32)
    m0 = lane < HEAD_DIM
    m0w = jnp.concatenate([m0, m0], axis=1)
    dirs = (
        (rf_ref, vf_ref, af_ref, lwf_ref, kf_ref, bf_ref, yf_ref, d > 0, d >= 0, False),
        (rb_ref, vb_ref, ab_ref, lwb_ref, kb_ref, bb_ref, yb_ref, d < 0, d <= 0, True),
    )

    def running_sum(x, backward):
        sh = 1
        while sh < c1:
            if backward:
                x = x + jnp.where(rid < c1 - sh, pltpu.roll(x, c1 - sh, axis=0), 0.0)
            else:
                x = x + jnp.where(rid >= sh, pltpu.roll(x, sh, axis=0), 0.0)
            sh *= 2
        return x

    def stack(x, mask=m0):
        zero = jnp.zeros_like(x)
        return jnp.concatenate([jnp.where(mask, x, zero), jnp.where(mask, zero, x)], axis=0)

    def chunk_body(ci, carry):
        st = []
        for z, (r_ref, v_ref, a_ref, lw_ref, k_ref, b_ref, y_ref, strict, incl, backward) in enumerate(dirs):
            c = ci if z == 0 else nchunk - 1 - ci
            sl = pl.ds(pl.multiple_of(c * CHUNK, CHUNK), CHUNK)
            lw_all = lw_ref[0, 0, sl, :]
            cl_all = running_sum(lw_all, backward)
            for t in range(nt):
                ls = slice(t * LANES, (t + 1) * LANES)
                lw = lw_all[:, ls]
                cl = cl_all[:, ls]
                e_in = jnp.exp(cl)
                e_ex = jnp.exp(cl - lw)
                e_neg = jnp.exp(-cl)
                q = dict(zi=z, t=t, sl=sl, ls=ls, y_ref=y_ref, strict=strict, incl=incl)
                tot = jnp.sum(lw, axis=0, keepdims=True)
                e_out = jnp.exp(tot - cl)
                bb, kd = b_ref[0, 0, sl, ls].astype(F32), k_ref[0, 0, sl, ls].astype(F32)
                q["wcol"] = jnp.broadcast_to(jnp.exp(tot), (HEADS_PER_TILE * c1, LANES)).T
                q["ar"] = jnp.concatenate([a_ref[0, sl, ls].astype(F32) * e_ex, r_ref[0, sl, ls].astype(F32) * e_in],
                                          axis=0).astype(BF16)
                q["bk"] = jnp.concatenate([stack((bb * e_neg).astype(BF16)), stack((kd * e_neg).astype(BF16))], axis=0)
                q["bk_t"] = jnp.concatenate([stack(bb * e_out).T, stack(kd * e_out).T], axis=1).astype(BF16)
                q["vs"] = stack(v_ref[0, sl, ls].astype(BF16))
                st.append(q)
        for q in st:
            q["g"] = _dot_nt(q["ar"], q["bk"])
        for q in st:
            g = q.pop("g")
            a_ab = jnp.where(q["strict"], g[:c1, :LANES], 0.0)
            q["a_ak"] = jnp.where(q["strict"], g[:c1, LANES:], 0.0).astype(BF16)
            q["a_r"] = jnp.concatenate([jnp.where(q["incl"], g[c1:, :LANES], 0.0),
                                        jnp.where(q["incl"], g[c1:, LANES:], 0.0)], axis=1).astype(BF16)
            q["tm"] = eye + a_ab
            q["pw"] = a_ab.astype(BF16)
        levels = CHUNK.bit_length() - 1
        for q in st:
            q["pw"] = _dot(q["pw"], stack(q["pw"])).astype(BF16)
        for _ in range(levels - 2):
            for q in st:
                pt = jnp.concatenate([q["pw"], q["tm"].astype(BF16)], axis=1)
                q["z"] = _dot(q["pw"], stack(pt, m0w))
            for q in st:
                z = q.pop("z")
                q["pw"] = z[:, :LANES].astype(BF16)
                q["tm"] = q["tm"] + z[:, LANES:]
        for q in st:
            q["tm"] = q["tm"] + _dot(q["pw"], stack(q["tm"].astype(BF16)))
        for q in st:
            q["s"] = s_ref[q["zi"], q["t"]]
            xy = _dot(q["ar"], q["s"].astype(BF16))
            q["y_s"] = xy[c1:]
            q["x"] = xy[:c1] + _dot(q["a_ak"], q["vs"])
        for q in st:
            q["u"] = stack(_dot(q["tm"].astype(BF16), stack(q["x"].astype(BF16))).astype(BF16))
        for q in st:
            uv = jnp.concatenate([q["u"], q["vs"]], axis=0)
            z = _dot(jnp.concatenate([q["a_r"], q["bk_t"]], axis=0), uv)
            q["y_ref"][0, q["sl"], q["ls"]] = q["y_s"] + z[:c1]
            s_ref[q["zi"], q["t"]] = q["s"] * q["wcol"] + z[c1:]
        return carry

    lax.fori_loop(0, nchunk, chunk_body, 0, unroll=2)


def _wkv_call(r, v, aa, lw, kd, bb, tb=512, nt=8):
    b, t, _ = r.shape
    tb = min(tb, t)
    ntb = t // tb
    w = nt * LANES
    fwd = pl.BlockSpec((1, tb, w), lambda bi, hi, j: (bi, j, hi))
    bwd = pl.BlockSpec((1, tb, w), lambda bi, hi, j: (bi, ntb - 1 - j, hi))
    fwd_d = pl.BlockSpec((1, 1, tb, w), lambda bi, hi, j: (0, bi, j, hi))
    bwd_d = pl.BlockSpec((1, 1, tb, w), lambda bi, hi, j: (1, bi, ntb - 1 - j, hi))
    out = jax.ShapeDtypeStruct((b, t, D_MODEL), F32)
    return pl.pallas_call(
        functools.partial(_wkv_kernel, tb=tb, nt=nt),
        grid=(b, N_HEAD_TILES // nt, ntb),
        in_specs=[fwd, fwd, fwd, bwd, bwd, bwd, fwd_d, fwd_d, fwd_d, bwd_d, bwd_d, bwd_d],
        out_specs=[fwd, bwd],
        out_shape=[out, out],
        scratch_shapes=[pltpu.VMEM((2, nt, LANES, LANES), F32)],
        compiler_params=_params("parallel", "parallel", "arbitrary"),
        name="wkv",
    )(r, v, aa, r, v, aa, lw, kd, bb, lw, kd, bb)


def _row(v):
    return v.reshape(1, -1).astype(F32)


def _pad_to(a, axis, size):
    pad = [(0, 0)] * a.ndim
    pad[axis] = (0, size - a.shape[axis])
    return jnp.pad(a, pad)


def _block_diag2(m):
    l, d = m.shape[1], m.shape[2]
    zero = jnp.zeros((l, d), m.dtype)
    return jnp.concatenate([jnp.concatenate([m[0], zero], axis=1), jnp.concatenate([zero, m[1]], axis=1)], axis=0)


def _round_up(n, m):
    return (n + m - 1) // m * m


def _prepare(w):
    hs = np.kron(np.eye(MXU_W // HEAD_DIM), np.ones((HEAD_DIM, HEAD_DIM)))
    hs = jnp.asarray(hs, BF16)
    na = []
    for li in range(w["na_w_qkv"].shape[0]):
        na.append(dict(
            w_qkv=w["na_w_qkv"][li].astype(BF16),
            w_o=w["na_w_o"][li].astype(BF16),
            gq=_row(jnp.tile(w["na_q_gain"][li], N_HEADS)) * (HEAD_DIM ** -0.5),
            gk=_row(jnp.tile(w["na_k_gain"][li], N_HEADS)),
            tbl=_na_bias_table(w["na_rpb"][li]),
        ))
    rw = []
    for li in range(w["rw_w_r"].shape[0]):
        lg = _round_up(w["rw_g1"].shape[-1], LANES)
        p = dict(
            mu=w["rw_mu"][li].astype(F32),
            w_r=w["rw_w_r"][li].astype(BF16), w_k=w["rw_w_k"][li].astype(BF16),
            w_v=w["rw_w_v"][li].astype(BF16), w_o=w["rw_w_o"][li].astype(BF16),
            w1=jnp.concatenate([w["rw_w1"][li, 0], w["rw_w1"][li, 1]], axis=1).astype(BF16),
            w2=(0.5 * _block_diag2(w["rw_w2"][li])).astype(BF16),
            a1=jnp.concatenate([w["rw_a1"][li, 0], w["rw_a1"][li, 1]], axis=1).astype(BF16),
            a2=(0.5 * _block_diag2(w["rw_a2"][li])).astype(BF16),
            g1=_pad_to(w["rw_g1"][li], 1, lg).astype(BF16),
            g2=_pad_to(w["rw_g2"][li], 0, lg).astype(BF16),
            w0=0.5 * _row(w["rw_w0"][li]), a0=0.5 * _row(w["rw_a0"][li]),
            k_k=_row(w["rw_k_k"][li]), r_k=_row(w["rw_r_k"][li]),
            k_a=jnp.concatenate([1.0 - 0.5 * _row(w["rw_k_a"][li]), 0.5 * _row(w["rw_k_a"][li])], axis=0),
            ln_w=_row(w["rw_ln_w"][li]), ln_b=_row(w["rw_ln_b"][li]),
            hs=hs,
        )
        if li >= 1:
            lv = _round_up(w["rw_v1"].shape[-1], LANES)
            p.update(v0=_row(w["rw_v0"][li - 1]),
                     v1=_pad_to(w["rw_v1"][li - 1], 1, lv).astype(BF16),
                     v2=_pad_to(w["rw_v2"][li - 1], 0, lv).astype(BF16))
        rw.append(p)
    ffn = []
    for i in range(DEPTH):
        ffn.append(dict(
            w_gate=w["ffn_w_gate"][i].astype(BF16), w_up=w["ffn_w_up"][i].astype(BF16),
            conv_w=w["ffn_conv_w"][i].astype(F32), conv_b=_row(w["ffn_conv_b"][i]),
            w_down=w["ffn_w_down"][i].astype(BF16),
        ))
    return dict(na=na, rw=rw, ffn=ffn, hs=hs,
                norm_mix=w["norm_mix"].astype(F32), norm_ffn=w["norm_ffn"].astype(F32))


def _trunk(x, p):
    b, t, d = x.shape
    n = b * t
    x2 = x.reshape(n, d)
    v_first = None
    for i in range(DEPTH):
        li = i // 2
        g_mix = _row(p["norm_mix"][i])
        if i % 2 == 0:
            a = p["na"][li]
            q, k, v = _qkv_call(x2, g_mix, a["w_qkv"], a["gq"], a["gk"], p["hs"])
            o = _na_call(q.reshape(b, t, d), k.reshape(b, t, d), v.reshape(b, t, d), a["tbl"])
            pre, pre_args = "proj", (o.reshape(n, d), a["w_o"])
        else:
            rp = dict(p["rw"][li], g=g_mix)
            r, v, aa, g, bonus, lw, kd, bb = _rwkv_in_call(x2, t, rp, v_first)
            if v_first is None:
                v_first = v
            yf, yb = _wkv_call(r.reshape(b, t, d), v.reshape(b, t, d), aa.reshape(b, t, d),
                               lw.reshape(2, b, t, d), kd.reshape(2, b, t, d), bb.reshape(2, b, t, d))
            pre, pre_args = "rwkv", (yf.reshape(n, d), yb.reshape(n, d), bonus, g, rp["ln_w"], rp["ln_b"], p["hs"],
                                     rp["w_o"])
        x2 = _ffn_call(x2, _row(p["norm_ffn"][i]), p["ffn"][i], t, pre, pre_args)
    return x2.reshape(b, t, d)


def kernel(x_prompt, x_sample, norm_mix, norm_ffn, na_w_qkv, na_w_o, na_q_gain, na_k_gain, na_rpb, rw_mu, rw_w_r, rw_w_k, rw_w_v, rw_w_o, rw_w0, rw_w1, rw_w2, rw_a0, rw_a1, rw_a2, rw_v0, rw_v1, rw_v2, rw_g1, rw_g2, rw_k_k, rw_k_a, rw_r_k, rw_ln_w, rw_ln_b, ffn_w_gate, ffn_w_up, ffn_conv_w, ffn_conv_b, ffn_w_down):
    w = dict(norm_mix=norm_mix, norm_ffn=norm_ffn, na_w_qkv=na_w_qkv, na_w_o=na_w_o, na_q_gain=na_q_gain,
             na_k_gain=na_k_gain, na_rpb=na_rpb, rw_mu=rw_mu, rw_w_r=rw_w_r, rw_w_k=rw_w_k, rw_w_v=rw_w_v,
             rw_w_o=rw_w_o, rw_w0=rw_w0, rw_w1=rw_w1, rw_w2=rw_w2, rw_a0=rw_a0, rw_a1=rw_a1, rw_a2=rw_a2,
             rw_v0=rw_v0, rw_v1=rw_v1, rw_v2=rw_v2, rw_g1=rw_g1, rw_g2=rw_g2, rw_k_k=rw_k_k, rw_k_a=rw_k_a,
             rw_r_k=rw_r_k, rw_ln_w=rw_ln_w, rw_ln_b=rw_ln_b, ffn_w_gate=ffn_w_gate, ffn_w_up=ffn_w_up,
             ffn_conv_w=ffn_conv_w, ffn_conv_b=ffn_conv_b, ffn_w_down=ffn_w_down)
    p = _prepare(w)
    return (_trunk(x_prompt, p), _trunk(x_sample, p))
```

```python
import functools
import math

import numpy as np
import jax
import jax.numpy as jnp
from jax import lax
from jax.experimental import pallas as pl
from jax.experimental.pallas import tpu as pltpu

F32 = jnp.float32
BF16 = jnp.bfloat16

D_MODEL = 1024
GRID_W = 64
N_HEADS = 16
HEAD_DIM = 64
WIN_H = 8
WIN_W = 16
RPB_H = 2 * WIN_H - 1
RPB_W = 2 * WIN_W - 1
D_FF = 2816
DEPTH = 4
RMS_EPS = 1e-6
GN_EPS = 64e-5
NEG_INF = -1e30

LANES = 128
HEADS_PER_TILE = LANES // HEAD_DIM
N_HEAD_TILES = D_MODEL // LANES
MXU_W = 256
N_MXU_TILES = D_MODEL // MXU_W
F32_SUBLANES = 8
HALO = 16
FF_CHUNK = MXU_W
CHUNK = 64
NA_ROWS_PER_STEP = 32
VMEM_LIMIT = 56 * 1024 * 1024


def _params(*sem):
    return pltpu.CompilerParams(dimension_semantics=sem, vmem_limit_bytes=VMEM_LIMIT)


def _const_spec(shape):
    nd = len(shape)
    return pl.BlockSpec(shape, lambda *_: (0,) * nd, pipeline_mode=pl.Buffered(1))


def _dot(a, b):
    return jnp.dot(a, b, preferred_element_type=F32)


def _dot_nt(a, b):
    return lax.dot_general(a, b, (((1,), (1,)), ((), ())), preferred_element_type=F32)


def _rms_rows(xf, g):
    ms = jnp.mean(xf * xf, axis=-1, keepdims=True)
    return xf * lax.rsqrt(ms + RMS_EPS) * g


def _head_sum(x_bf16, hsum):
    return _dot(x_bf16, hsum)


def _sigmoid(x):
    return 0.5 * jnp.tanh(0.5 * x) + 0.5


def _split2(x):
    hi = x.astype(BF16)
    lo = (x - hi.astype(F32)).astype(BF16)
    return hi, lo


def _qkv_kernel(x_ref, g_ref, w_ref, gq_ref, gk_ref, hs_ref, q_ref, k_ref, v_ref):
    hn = _rms_rows(x_ref[...], g_ref[...]).astype(BF16)
    hs = hs_ref[...]
    for part, (o_ref, gain_ref) in enumerate(((q_ref, gq_ref), (k_ref, gk_ref))):
        y = _dot(hn, w_ref[:, part * D_MODEL:(part + 1) * D_MODEL])
        gain = gain_ref[...]
        for t in range(N_MXU_TILES):
            sl = slice(t * MXU_W, (t + 1) * MXU_W)
            ys = y[:, sl]
            ms = _head_sum((ys * ys).astype(BF16), hs) * (1.0 / HEAD_DIM)
            o_ref[:, sl] = (ys * lax.rsqrt(ms + RMS_EPS) * gain[:, sl]).astype(BF16)
    v_ref[...] = _dot(hn, w_ref[:, 2 * D_MODEL:]).astype(BF16)


def _qkv_call(x2, g, w_qkv, gq, gk, hs, tm=512):
    n = x2.shape[0]
    tok = pl.BlockSpec((tm, D_MODEL), lambda i: (i, 0))
    out = jax.ShapeDtypeStruct((n, D_MODEL), BF16)
    return pl.pallas_call(
        _qkv_kernel,
        grid=(n // tm,),
        in_specs=[tok, _const_spec((1, D_MODEL)), _const_spec((D_MODEL, 3 * D_MODEL)),
                  _const_spec((1, D_MODEL)), _const_spec((1, D_MODEL)), _const_spec((MXU_W, MXU_W))],
        out_specs=[tok, tok, tok],
        out_shape=[out, out, out],
        compiler_params=_params("parallel"),
        name="na_qkv",
    )(x2, g, w_qkv, gq, gk, hs)


def _na_kernel(q_ref, k_ref, v_ref, tbl_ref, o_ref, *, rows):
    kh = min(WIN_H, rows)
    lane = lax.broadcasted_iota(jnp.int32, (GRID_W, LANES), 1)
    head_mask = [lane < HEAD_DIM, lane >= HEAD_DIM]

    def rows_body(rb, carry):
        st = []
        for ri in range(NA_ROWS_PER_STEP):
            r = rb * NA_ROWS_PER_STEP + ri
            rs = jnp.clip(r - kh // 2, 0, rows - kh)
            qsl = pl.ds(pl.multiple_of(r * GRID_W, GRID_W), GRID_W)
            ksl = pl.ds(pl.multiple_of(rs * GRID_W, GRID_W), kh * GRID_W)
            q = q_ref[0, qsl, :]
            zero = jnp.zeros_like(q)
            qs = jnp.concatenate([jnp.where(head_mask[0], q, zero), jnp.where(head_mask[1], q, zero)], axis=0)
            st.append(dict(dr0=rs - r + WIN_H - 1, ksl=ksl, qsl=qsl, s=_dot_nt(qs, k_ref[0, ksl, :])))
        for e in st:
            bias = jnp.concatenate([tbl_ref[0, e["dr0"] + 2 * p] for p in range(kh // 2)], axis=1)
            s = e.pop("s") + bias
            m = jnp.max(s, axis=1, keepdims=True)
            p = jnp.exp(s - m)
            e["l"] = jnp.sum(p, axis=1, keepdims=True)
            e["p"] = p.astype(BF16)
        for e in st:
            e["o"] = _dot(e.pop("p"), v_ref[0, e["ksl"], :]) / e["l"]
        for e in st:
            o = e["o"]
            o_ref[0, e["qsl"], :] = jnp.where(head_mask[0], o[:GRID_W], o[GRID_W:]).astype(BF16)
        return carry

    lax.fori_loop(0, rows // NA_ROWS_PER_STEP, rows_body, 0)


def _na_call(q, k, v, tbl):
    b, t, _ = q.shape
    rows = t // GRID_W
    assert rows >= WIN_H and WIN_H % 2 == 0 and rows % NA_ROWS_PER_STEP == 0
    blk = pl.BlockSpec((1, t, LANES), lambda bi, hi: (bi, 0, hi))
    return pl.pallas_call(
        functools.partial(_na_kernel, rows=rows),
        grid=(b, N_HEAD_TILES),
        in_specs=[blk, blk, blk,
                  pl.BlockSpec((1, RPB_H - 1, HEADS_PER_TILE * GRID_W, LANES), lambda bi, hi: (hi, 0, 0, 0))],
        out_specs=blk,
        out_shape=jax.ShapeDtypeStruct((b, t, D_MODEL), BF16),
        compiler_params=_params("parallel", "parallel"),
        name="na_attn",
    )(q, k, v, tbl)


def _na_bias_table(rpb):
    tile, d, hh, c, two, kc = np.ix_(*(np.arange(n) for n in (N_HEAD_TILES, RPB_H - 1, HEADS_PER_TILE, GRID_W, 2, GRID_W)))
    cs = np.clip(c - WIN_W // 2, 0, GRID_W - WIN_W)
    ok = (kc >= cs) & (kc < cs + WIN_W)
    flat = ((tile * HEADS_PER_TILE + hh) * RPB_H + d + two) * RPB_W + np.clip(kc - c + WIN_W - 1, 0, RPB_W - 1)
    idx = np.where(ok, flat, N_HEADS * RPB_H * RPB_W)
    idx = idx.reshape(N_HEAD_TILES, RPB_H - 1, HEADS_PER_TILE * GRID_W, LANES).astype(np.int32)
    source = jnp.concatenate([rpb.astype(F32).reshape(-1), jnp.full((1,), NEG_INF, F32)])
    return source[idx]


def _shift_rows(cur, prev_row, next_row, tm):
    rid = lax.broadcasted_iota(jnp.int32, (F32_SUBLANES, 1), 0)
    down = pltpu.roll(cur, 1, axis=0)
    up = pltpu.roll(cur, tm - 1, axis=0)
    prev = jnp.concatenate([jnp.where(rid == 0, prev_row, down[:F32_SUBLANES]), down[F32_SUBLANES:]], axis=0)
    nxt = jnp.concatenate([up[:tm - F32_SUBLANES],
                           jnp.where(rid == F32_SUBLANES - 1, next_row, up[tm - F32_SUBLANES:])], axis=0)
    return prev, nxt


def _ext(refs3):
    return jnp.concatenate([r[...] for r in refs3], axis=0)


def _pre_phases(pre, blocks, consts):
    st = {}
    if pre == "proj":
        a3, x3 = blocks
        (wo_ref,) = consts

        def last():
            return _ext(x3) + _dot(_ext(a3), wo_ref[...])

        return [lambda: None, lambda: None, last]

    yf3, yb3, bonus3, gate3, x3 = blocks
    lnw_ref, lnb_ref, hs_ref, wo_ref = consts
    tiles = [slice(t * MXU_W, (t + 1) * MXU_W) for t in range(N_MXU_TILES)]

    def mean():
        st["y"] = _ext(yf3) + _ext(yb3)
        sums = []
        for sl in tiles:
            hi, lo = _split2(st["y"][:, sl])
            sums.append(_head_sum(hi, hs_ref[...]) + _head_sum(lo, hs_ref[...]))
        st["sum"] = sums

    def var():
        st["dlt"] = [st["y"][:, sl] - sm * (1.0 / HEAD_DIM) for sl, sm in zip(tiles, st["sum"])]
        st["sq"] = [_head_sum((d * d).astype(BF16), hs_ref[...]) for d in st["dlt"]]

    def last():
        yn = jnp.concatenate([d * lax.rsqrt(sq * (1.0 / HEAD_DIM) + GN_EPS) for d, sq in zip(st["dlt"], st["sq"])],
                             axis=1)
        yn = yn * lnw_ref[...] + lnb_ref[...]
        zz = ((yn + _ext(bonus3).astype(F32)) * _ext(gate3).astype(F32)).astype(BF16)
        return _ext(x3) + _dot(zz, wo_ref[...])

    return [mean, var, last]


PRE_AFTER_CHUNK = (1, 3, 6)


def _ffn_kernel(*refs, tm, seq, pre):
    n_tiled, n_const = (2, 1) if pre == "proj" else (5, 4)
    it = iter(refs)

    def take(k):
        return [next(it) for _ in range(k)]

    first_blocks = [take(3) for _ in range(n_tiled)]
    next_blocks = [take(3) for _ in range(n_tiled)]
    consts = take(n_const)
    g_ref, wg_ref, wu_ref, cw_ref, cb_ref, wd_ref, o_ref, xs_ref = take(8)

    @pl.when(pl.program_id(0) == 0)
    def _():
        phases = _pre_phases(pre, first_blocks, consts)
        phases[0]()
        phases[1]()
        xs_ref[...] = phases[2]()

    t0 = pl.program_id(0) * tm
    first = (t0 % seq) == 0
    last = ((t0 + tm) % seq) == 0
    x_ext = xs_ref[...]
    rid = lax.broadcasted_iota(jnp.int32, (tm + 2 * HALO, 1), 0)
    outside = jnp.logical_or(jnp.logical_and(first, rid < HALO), jnp.logical_and(last, rid >= tm + HALO))
    h_ext = jnp.where(outside, 0.0, _rms_rows(x_ext, g_ref[...]))
    s = F32_SUBLANES
    ext = tm + 2 * s
    hn_ext = h_ext[HALO - s:HALO + tm + s].astype(BF16)
    hn = h_ext[HALO:HALO + tm].astype(BF16)
    cw = 0.5 * cw_ref[...]
    cb = 0.5 * cb_ref[...]

    def gate_up(c):
        cs = slice(c * FF_CHUNK, (c + 1) * FF_CHUNK)
        return _dot(hn_ext, wg_ref[:, cs]), _dot(hn, wu_ref[:, cs])

    next_phases = _pre_phases(pre, next_blocks, consts)
    x_next = None
    acc = x_ext[HALO:HALO + tm]
    nxt = gate_up(0)
    for c in range(D_FF // FF_CHUNK):
        cs = slice(c * FF_CHUNK, (c + 1) * FF_CHUNK)
        ge, up = nxt
        if c + 1 < D_FF // FF_CHUNK:
            nxt = gate_up(c + 1)
        gprev = pltpu.roll(ge, 1, axis=0)[s:s + tm]
        gnext = pltpu.roll(ge, ext - 1, axis=0)[s:s + tm]
        h = gprev * cw[0:1, cs] + ge[s:s + tm] * cw[1:2, cs] + gnext * cw[2:3, cs] + cb[:, cs]
        act = (h * (jnp.tanh(h) + 1.0) * up).astype(BF16)
        acc = acc + _dot(act, wd_ref[cs, :])
        if c in PRE_AFTER_CHUNK:
            x_next = next_phases[PRE_AFTER_CHUNK.index(c)]()
    o_ref[...] = acc
    xs_ref[...] = x_next


def _halo_specs(tm, n, width, tile_of_step, single=False):
    hb = tm // HALO
    nhb = n // HALO
    mode = dict(pipeline_mode=pl.Buffered(1)) if single else {}
    return [pl.BlockSpec((HALO, width), lambda i: (jnp.maximum(tile_of_step(i) * hb - 1, 0), 0), **mode),
            pl.BlockSpec((tm, width), lambda i: (tile_of_step(i), 0), **mode),
            pl.BlockSpec((HALO, width), lambda i: (jnp.minimum((tile_of_step(i) + 1) * hb, nhb - 1), 0), **mode)]


def _ffn_call(x2, g, f, seq, pre, pre_args, tm=256):
    n = x2.shape[0]
    ntiles = n // tm
    if pre == "proj":
        a, w_o = pre_args
        tiled, consts = [a, x2], [w_o]
    else:
        yf, yb, bonus, gate, ln_w, ln_b, hs, w_o = pre_args
        tiled, consts = [yf, yb, bonus, gate, x2], [ln_w, ln_b, hs, w_o]
    consts = consts + [g, f["w_gate"], f["w_up"], f["conv_w"], f["conv_b"], f["w_down"]]
    args, specs = [], []
    for tile_of_step, single in ((lambda i: i * 0, True), (lambda i: jnp.minimum(i + 1, ntiles - 1), False)):
        for a in tiled:
            args.extend([a, a, a])
            specs.extend(_halo_specs(tm, n, a.shape[1], tile_of_step, single))
    for a in consts:
        args.append(a)
        specs.append(_const_spec(a.shape))
    return pl.pallas_call(
        functools.partial(_ffn_kernel, tm=tm, seq=seq, pre=pre),
        grid=(ntiles,),
        in_specs=specs,
        out_specs=pl.BlockSpec((tm, D_MODEL), lambda i: (i, 0)),
        out_shape=jax.ShapeDtypeStruct((n, D_MODEL), F32),
        scratch_shapes=[pltpu.VMEM((tm + 2 * HALO, D_MODEL), F32)],
        compiler_params=_params("arbitrary"),
        name="ffn_" + pre,
    )(*args)


def _rwkv_in_kernel(*refs, tm, seq, has_vres):
    (x_ref, xp_ref, xn_ref, g_ref, mu_ref, wr_ref, wk_ref, wv_ref, w1_ref, w2_ref, a1_ref, a2_ref,
     g1_ref, g2_ref, w0_ref, a0_ref, kk_ref, ka_ref, rk_ref, hs_ref) = refs[:20]
    refs = refs[20:]
    if has_vres:
        vf_ref, v0_ref, v1_ref, v2_ref = refs[:4]
        refs = refs[4:]
    r_out, v_out, aa_out, g_out, bonus_out, lw_out, kd_out, bb_out = refs

    t0 = pl.program_id(0) * tm
    first = (t0 % seq) == 0
    last = ((t0 + tm) % seq) == 0
    gn = g_ref[...]
    h = _rms_rows(x_ref[...], gn)
    prev_row = jnp.where(first, 0.0, _rms_rows(xp_ref[...], gn)[F32_SUBLANES - 1:F32_SUBLANES, :])
    next_row = jnp.where(last, 0.0, _rms_rows(xn_ref[...], gn)[0:1, :])
    prev, nxt = _shift_rows(h, prev_row, next_row, tm)
    xx = 0.5 * (prev + nxt) - h
    mu = mu_ref[...]

    def mix(i):
        return (h + xx * mu[i:i + 1, :]).astype(BF16)

    hs = hs_ref[...]
    tiles = [slice(t * MXU_W, (t + 1) * MXU_W) for t in range(N_MXU_TILES)]

    la = _dot(_dot(mix(4), a1_ref[...]).astype(BF16), a2_ref[...])
    lwl = _dot(jnp.tanh(_dot(mix(1), w1_ref[...])).astype(BF16), w2_ref[...])
    k = _dot(mix(2), wk_ref[...])
    kk = k * kk_ref[...]
    kkn = jnp.concatenate(
        [kk[:, sl] * jnp.minimum(lax.rsqrt(_head_sum((kk[:, sl] * kk[:, sl]).astype(BF16), hs)), 1e12)
         for sl in tiles], axis=1)
    aa_out[...] = (-kkn).astype(BF16)
    ka1, ka2 = ka_ref[0:1, :], ka_ref[1:2, :]
    half_decay = -0.5 * math.exp(-0.5)
    kd_sum = None
    for z in range(2):
        zs = slice(z * D_MODEL, (z + 1) * D_MODEL)
        lw_out[z] = half_decay * jnp.tanh(w0_ref[:, zs] + lwl[:, zs]) + half_decay
        th = jnp.tanh(a0_ref[:, zs] + la[:, zs])
        a = 0.5 * th + 0.5
        kd = k * (ka1 + ka2 * th)
        kd_out[z] = kd.astype(BF16)
        bb_out[z] = (kkn * a).astype(BF16)
        kd_sum = kd if kd_sum is None else kd_sum + kd

    r = _dot(mix(0), wr_ref[...])
    r_out[...] = r.astype(BF16)
    rkk = r * kd_sum * rk_ref[...]
    rk_heads = []
    for sl in tiles:
        hi, lo = _split2(rkk[:, sl])
        rk_heads.append(_head_sum(hi, hs) + _head_sum(lo, hs))
    xv = mix(3)
    v = _dot(xv, wv_ref[...])
    if has_vres:
        vl = _dot(_dot(xv, v1_ref[...]).astype(BF16), v2_ref[...])
        v = v + (vf_ref[...] - v) * _sigmoid(v0_ref[...] + vl)
    v_out[...] = v
    bonus_out[...] = (jnp.concatenate(rk_heads, axis=1) * v).astype(BF16)
    g_out[...] = _dot(_sigmoid(_dot(mix(5), g1_ref[...])).astype(BF16), g2_ref[...]).astype(BF16)


def _rwkv_in_call(x2, seq, p, v_first, tm=256):
    n = x2.shape[0]
    has_vres = v_first is not None
    hb = tm // F32_SUBLANES
    nhb = n // F32_SUBLANES
    tok = pl.BlockSpec((tm, D_MODEL), lambda i: (i, 0))
    tok2 = pl.BlockSpec((2, tm, D_MODEL), lambda i: (0, i, 0))
    halo_p = pl.BlockSpec((F32_SUBLANES, D_MODEL), lambda i: (jnp.maximum(i * hb - 1, 0), 0))
    halo_n = pl.BlockSpec((F32_SUBLANES, D_MODEL), lambda i: (jnp.minimum((i + 1) * hb, nhb - 1), 0))
    args = [x2, x2, x2, p["g"], p["mu"], p["w_r"], p["w_k"], p["w_v"], p["w1"], p["w2"], p["a1"], p["a2"],
            p["g1"], p["g2"], p["w0"], p["a0"], p["k_k"], p["k_a"], p["r_k"], p["hs"]]
    specs = [tok, halo_p, halo_n] + [_const_spec(a.shape) for a in args[3:]]
    if has_vres:
        extra = [v_first, p["v0"], p["v1"], p["v2"]]
        args += extra
        specs += [tok] + [_const_spec(a.shape) for a in extra[1:]]
    def one(dt):
        return jax.ShapeDtypeStruct((n, D_MODEL), dt)

    def two(dt):
        return jax.ShapeDtypeStruct((2, n, D_MODEL), dt)

    return pl.pallas_call(
        functools.partial(_rwkv_in_kernel, tm=tm, seq=seq, has_vres=has_vres),
        grid=(n // tm,),
        in_specs=specs,
        out_specs=[tok] * 5 + [tok2] * 3,
        out_shape=[one(BF16), one(F32), one(BF16), one(BF16), one(BF16), two(F32), two(BF16), two(BF16)],
        compiler_params=_params("parallel"),
        name="rwkv_in",
    )(*args)


def _wkv_kernel(rf_ref, vf_ref, af_ref, rb_ref, vb_ref, ab_ref, lwf_ref, kf_ref, bf_ref, lwb_ref, kb_ref, bb_ref,
                yf_ref, yb_ref, s_ref, *, tb, nt):
    assert CHUNK == HEAD_DIM
    nchunk = tb // CHUNK
    c1 = CHUNK

    @pl.when(pl.program_id(2) == 0)
    def _():
        s_ref[...] = jnp.zeros_like(s_ref)

    lane = lax.broadcasted_iota(jnp.int32, (c1, LANES), 1)
    d = lax.broadcasted_iota(jnp.int32, (c1, LANES), 0) - (lane & (c1 - 1))
    rid = lax.broadcasted_iota(jnp.int32, (c1, 1), 0)
    eye = (d == 0).astype(F32)
    m0 = lane < HEAD_DIM
    m0w = jnp.concatenate([m0, m0], axis=1)
    dirs = (
        (rf_ref, vf_ref, af_ref, lwf_ref, kf_ref, bf_ref, yf_ref, d > 0, d >= 0, False),
        (rb_ref, vb_ref, ab_ref, lwb_ref, kb_ref, bb_ref, yb_ref, d < 0, d <= 0, True),
    )

    def running_sum(x, backward):
        sh = 1
        while sh < c1:
            if backward:
                x = x + jnp.where(rid < c1 - sh, pltpu.roll(x, c1 - sh, axis=0), 0.0)
            else:
                x = x + jnp.where(rid >= sh, pltpu.roll(x, sh, axis=0), 0.0)
            sh *= 2
        return x

    def stack(x, mask=m0):
        zero = jnp.zeros_like(x)
        return jnp.concatenate([jnp.where(mask, x, zero), jnp.where(mask, zero, x)], axis=0)

    def chunk_body(ci, carry):
        st = []
        for z, (r_ref, v_ref, a_ref, lw_ref, k_ref, b_ref, y_ref, strict, incl, backward) in enumerate(dirs):
            c = ci if z == 0 else nchunk - 1 - ci
            sl = pl.ds(pl.multiple_of(c * CHUNK, CHUNK), CHUNK)
            lw_all = lw_ref[0, 0, sl, :]
            cl_all = running_sum(lw_all, backward)
            for t in range(nt):
                ls = slice(t * LANES, (t + 1) * LANES)
                lw = lw_all[:, ls]
                cl = cl_all[:, ls]
                e_in = jnp.exp(cl)
                e_ex = jnp.exp(cl - lw)
                e_neg = jnp.exp(-cl)
                q = dict(zi=z, t=t, sl=sl, ls=ls, y_ref=y_ref, strict=strict, incl=incl)
                tot = jnp.sum(lw, axis=0, keepdims=True)
                e_out = jnp.exp(tot - cl)
                bb, kd = b_ref[0, 0, sl, ls].astype(F32), k_ref[0, 0, sl, ls].astype(F32)
                q["wcol"] = jnp.broadcast_to(jnp.exp(tot), (HEADS_PER_TILE * c1, LANES)).T
                q["ar"] = jnp.concatenate([a_ref[0, sl, ls].astype(F32) * e_ex, r_ref[0, sl, ls].astype(F32) * e_in],
                                          axis=0).astype(BF16)
                q["bk"] = jnp.concatenate([stack((bb * e_neg).astype(BF16)), stack((kd * e_neg).astype(BF16))], axis=0)
                q["bk_t"] = jnp.concatenate([stack(bb * e_out).T, stack(kd * e_out).T], axis=1).astype(BF16)
                q["vs"] = stack(v_ref[0, sl, ls].astype(BF16))
                st.append(q)
        for q in st:
            q["g"] = _dot_nt(q["ar"], q["bk"])
        for q in st:
            g = q.pop("g")
            a_ab = jnp.where(q["strict"], g[:c1, :LANES], 0.0)
            q["a_ak"] = jnp.where(q["strict"], g[:c1, LANES:], 0.0).astype(BF16)
            q["a_r"] = jnp.concatenate([jnp.where(q["incl"], g[c1:, :LANES], 0.0),
                                        jnp.where(q["incl"], g[c1:, LANES:], 0.0)], axis=1).astype(BF16)
            q["tm"] = eye + a_ab
            q["pw"] = a_ab.astype(BF16)
        levels = CHUNK.bit_length() - 1
        for q in st:
            q["pw"] = _dot(q["pw"], stack(q["pw"])).astype(BF16)
        for _ in range(levels - 2):
            for q in st:
                pt = jnp.concatenate([q["pw"], q["tm"].astype(BF16)], axis=1)
                q["z"] = _dot(q["pw"], stack(pt, m0w))
            for q in st:
                z = q.pop("z")
                q["pw"] = z[:, :LANES].astype(BF16)
                q["tm"] = q["tm"] + z[:, LANES:]
        for q in st:
            q["tm"] = q["tm"] + _dot(q["pw"], stack(q["tm"].astype(BF16)))
        for q in st:
            q["s"] = s_ref[q["zi"], q["t"]]
            xy = _dot(q["ar"], q["s"].astype(BF16))
            q["y_s"] = xy[c1:]
            q["x"] = xy[:c1] + _dot(q["a_ak"], q["vs"])
        for q in st:
            q["u"] = stack(_dot(q["tm"].astype(BF16), stack(q["x"].astype(BF16))).astype(BF16))
        for q in st:
            uv = jnp.concatenate([q["u"], q["vs"]], axis=0)
            z = _dot(jnp.concatenate([q["a_r"], q["bk_t"]], axis=0), uv)
            q["y_ref"][0, q["sl"], q["ls"]] = q["y_s"] + z[:c1]
            s_ref[q["zi"], q["t"]] = q["s"] * q["wcol"] + z[c1:]
        return carry

    lax.fori_loop(0, nchunk, chunk_body, 0, unroll=4)


def _wkv_call(r, v, aa, lw, kd, bb, tb=512, nt=8):
    b, t, _ = r.shape
    tb = min(tb, t)
    ntb = t // tb
    w = nt * LANES
    fwd = pl.BlockSpec((1, tb, w), lambda bi, hi, j: (bi, j, hi))
    bwd = pl.BlockSpec((1, tb, w), lambda bi, hi, j: (bi, ntb - 1 - j, hi))
    fwd_d = pl.BlockSpec((1, 1, tb, w), lambda bi, hi, j: (0, bi, j, hi))
    bwd_d = pl.BlockSpec((1, 1, tb, w), lambda bi, hi, j: (1, bi, ntb - 1 - j, hi))
    out = jax.ShapeDtypeStruct((b, t, D_MODEL), F32)
    return pl.pallas_call(
        functools.partial(_wkv_kernel, tb=tb, nt=nt),
        grid=(b, N_HEAD_TILES // nt, ntb),
        in_specs=[fwd, fwd, fwd, bwd, bwd, bwd, fwd_d, fwd_d, fwd_d, bwd_d, bwd_d, bwd_d],
        out_specs=[fwd, bwd],
        out_shape=[out, out],
        scratch_shapes=[pltpu.VMEM((2, nt, LANES, LANES), F32)],
        compiler_params=_params("parallel", "parallel", "arbitrary"),
        name="wkv",
    )(r, v, aa, r, v, aa, lw, kd, bb, lw, kd, bb)


def _row(v):
    return v.reshape(1, -1).astype(F32)


def _pad_to(a, axis, size):
    pad = [(0, 0)] * a.ndim
    pad[axis] = (0, size - a.shape[axis])
    return jnp.pad(a, pad)


def _block_diag2(m):
    l, d = m.shape[1], m.shape[2]
    zero = jnp.zeros((l, d), m.dtype)
    return jnp.concatenate([jnp.concatenate([m[0], zero], axis=1), jnp.concatenate([zero, m[1]], axis=1)], axis=0)


def _round_up(n, m):
    return (n + m - 1) // m * m


def _prepare(w):
    hs = np.kron(np.eye(MXU_W // HEAD_DIM), np.ones((HEAD_DIM, HEAD_DIM)))
    hs = jnp.asarray(hs, BF16)
    na = []
    for li in range(w["na_w_qkv"].shape[0]):
        na.append(dict(
            w_qkv=w["na_w_qkv"][li].astype(BF16),
            w_o=w["na_w_o"][li].astype(BF16),
            gq=_row(jnp.tile(w["na_q_gain"][li], N_HEADS)) * (HEAD_DIM ** -0.5),
            gk=_row(jnp.tile(w["na_k_gain"][li], N_HEADS)),
            tbl=_na_bias_table(w["na_rpb"][li]),
        ))
    rw = []
    for li in range(w["rw_w_r"].shape[0]):
        lg = _round_up(w["rw_g1"].shape[-1], LANES)
        p = dict(
            mu=w["rw_mu"][li].astype(F32),
            w_r=w["rw_w_r"][li].astype(BF16), w_k=w["rw_w_k"][li].astype(BF16),
            w_v=w["rw_w_v"][li].astype(BF16), w_o=w["rw_w_o"][li].astype(BF16),
            w1=jnp.concatenate([w["rw_w1"][li, 0], w["rw_w1"][li, 1]], axis=1).astype(BF16),
            w2=(0.5 * _block_diag2(w["rw_w2"][li])).astype(BF16),
            a1=jnp.concatenate([w["rw_a1"][li, 0], w["rw_a1"][li, 1]], axis=1).astype(BF16),
            a2=(0.5 * _block_diag2(w["rw_a2"][li])).astype(BF16),
            g1=_pad_to(w["rw_g1"][li], 1, lg).astype(BF16),
            g2=_pad_to(w["rw_g2"][li], 0, lg).astype(BF16),
            w0=0.5 * _row(w["rw_w0"][li]), a0=0.5 * _row(w["rw_a0"][li]),
            k_k=_row(w["rw_k_k"][li]), r_k=_row(w["rw_r_k"][li]),
            k_a=jnp.concatenate([1.0 - 0.5 * _row(w["rw_k_a"][li]), 0.5 * _row(w["rw_k_a"][li])], axis=0),
            ln_w=_row(w["rw_ln_w"][li]), ln_b=_row(w["rw_ln_b"][li]),
            hs=hs,
        )
        if li >= 1:
            lv = _round_up(w["rw_v1"].shape[-1], LANES)
            p.update(v0=_row(w["rw_v0"][li - 1]),
                     v1=_pad_to(w["rw_v1"][li - 1], 1, lv).astype(BF16),
                     v2=_pad_to(w["rw_v2"][li - 1], 0, lv).astype(BF16))
        rw.append(p)
    ffn = []
    for i in range(DEPTH):
        ffn.append(dict(
            w_gate=w["ffn_w_gate"][i].astype(BF16), w_up=w["ffn_w_up"][i].astype(BF16),
            conv_w=w["ffn_conv_w"][i].astype(F32), conv_b=_row(w["ffn_conv_b"][i]),
            w_down=w["ffn_w_down"][i].astype(BF16),
        ))
    return dict(na=na, rw=rw, ffn=ffn, hs=hs,
                norm_mix=w["norm_mix"].astype(F32), norm_ffn=w["norm_ffn"].astype(F32))


def _trunk(x, p):
    b, t, d = x.shape
    n = b * t
    x2 = x.reshape(n, d)
    v_first = None
    for i in range(DEPTH):
        li = i // 2
        g_mix = _row(p["norm_mix"][i])
        if i % 2 == 0:
            a = p["na"][li]
            q, k, v = _qkv_call(x2, g_mix, a["w_qkv"], a["gq"], a["gk"], p["hs"])
            o = _na_call(q.reshape(b, t, d), k.reshape(b, t, d), v.reshape(b, t, d), a["tbl"])
            pre, pre_args = "proj", (o.reshape(n, d), a["w_o"])
        else:
            rp = dict(p["rw"][li], g=g_mix)
            r, v, aa, g, bonus, lw, kd, bb = _rwkv_in_call(x2, t, rp, v_first)
            if v_first is None:
                v_first = v
            yf, yb = _wkv_call(r.reshape(b, t, d), v.reshape(b, t, d), aa.reshape(b, t, d),
                               lw.reshape(2, b, t, d), kd.reshape(2, b, t, d), bb.reshape(2, b, t, d))
            pre, pre_args = "rwkv", (yf.reshape(n, d), yb.reshape(n, d), bonus, g, rp["ln_w"], rp["ln_b"], p["hs"],
                                     rp["w_o"])
        x2 = _ffn_call(x2, _row(p["norm_ffn"][i]), p["ffn"][i], t, pre, pre_args)
    return x2.reshape(b, t, d)


def kernel(x_prompt, x_sample, norm_mix, norm_ffn, na_w_qkv, na_w_o, na_q_gain, na_k_gain, na_rpb, rw_mu, rw_w_r, rw_w_k, rw_w_v, rw_w_o, rw_w0, rw_w1, rw_w2, rw_a0, rw_a1, rw_a2, rw_v0, rw_v1, rw_v2, rw_g1, rw_g2, rw_k_k, rw_k_a, rw_r_k, rw_ln_w, rw_ln_b, ffn_w_gate, ffn_w_up, ffn_conv_w, ffn_conv_b, ffn_w_down):
    w = dict(norm_mix=norm_mix, norm_ffn=norm_ffn, na_w_qkv=na_w_qkv, na_w_o=na_w_o, na_q_gain=na_q_gain,
             na_k_gain=na_k_gain, na_rpb=na_rpb, rw_mu=rw_mu, rw_w_r=rw_w_r, rw_w_k=rw_w_k, rw_w_v=rw_w_v,
             rw_w_o=rw_w_o, rw_w0=rw_w0, rw_w1=rw_w1, rw_w2=rw_w2, rw_a0=rw_a0, rw_a1=rw_a1, rw_a2=rw_a2,
             rw_v0=rw_v0, rw_v1=rw_v1, rw_v2=rw_v2, rw_g1=rw_g1, rw_g2=rw_g2, rw_k_k=rw_k_k, rw_k_a=rw_k_a,
             rw_r_k=rw_r_k, rw_ln_w=rw_ln_w, rw_ln_b=rw_ln_b, ffn_w_gate=ffn_w_gate, ffn_w_up=ffn_w_up,
             ffn_conv_w=ffn_conv_w, ffn_conv_b=ffn_conv_b, ffn_w_down=ffn_w_down)
    p = _prepare(w)
    return (_trunk(x_prompt, p), _trunk(x_sample, p))
```

```python
import functools
import math

import numpy as np
import jax
import jax.numpy as jnp
from jax import lax
from jax.experimental import pallas as pl
from jax.experimental.pallas import tpu as pltpu

F32 = jnp.float32
BF16 = jnp.bfloat16

D_MODEL = 1024
GRID_W = 64
N_HEADS = 16
HEAD_DIM = 64
WIN_H = 8
WIN_W = 16
RPB_H = 2 * WIN_H - 1
RPB_W = 2 * WIN_W - 1
D_FF = 2816
DEPTH = 4
RMS_EPS = 1e-6
GN_EPS = 64e-5
NEG_INF = -1e30

LANES = 128
HEADS_PER_TILE = LANES // HEAD_DIM
N_HEAD_TILES = D_MODEL // LANES
MXU_W = 256
N_MXU_TILES = D_MODEL // MXU_W
F32_SUBLANES = 8
HALO = 16
FF_CHUNK = MXU_W
CHUNK = 64
NA_ROWS_PER_STEP = 32
VMEM_LIMIT = 56 * 1024 * 1024


def _params(*sem):
    return pltpu.CompilerParams(dimension_semantics=sem, vmem_limit_bytes=VMEM_LIMIT)


def _const_spec(shape):
    nd = len(shape)
    return pl.BlockSpec(shape, lambda *_: (0,) * nd, pipeline_mode=pl.Buffered(1))


def _dot(a, b):
    return jnp.dot(a, b, preferred_element_type=F32)


def _dot_nt(a, b):
    return lax.dot_general(a, b, (((1,), (1,)), ((), ())), preferred_element_type=F32)


def _rms_rows(xf, g):
    ms = jnp.mean(xf * xf, axis=-1, keepdims=True)
    return xf * lax.rsqrt(ms + RMS_EPS) * g


def _head_sum(x_bf16, hsum):
    return _dot(x_bf16, hsum)


def _sigmoid(x):
    return 0.5 * jnp.tanh(0.5 * x) + 0.5


def _split2(x):
    hi = x.astype(BF16)
    lo = (x - hi.astype(F32)).astype(BF16)
    return hi, lo


def _qkv_kernel(x_ref, g_ref, w_ref, gq_ref, gk_ref, hs_ref, q_ref, k_ref, v_ref):
    hn = _rms_rows(x_ref[...], g_ref[...]).astype(BF16)
    hs = hs_ref[...]
    for part, (o_ref, gain_ref) in enumerate(((q_ref, gq_ref), (k_ref, gk_ref))):
        y = _dot(hn, w_ref[:, part * D_MODEL:(part + 1) * D_MODEL])
        gain = gain_ref[...]
        for t in range(N_MXU_TILES):
            sl = slice(t * MXU_W, (t + 1) * MXU_W)
            ys = y[:, sl]
            ms = _head_sum((ys * ys).astype(BF16), hs) * (1.0 / HEAD_DIM)
            o_ref[:, sl] = (ys * lax.rsqrt(ms + RMS_EPS) * gain[:, sl]).astype(BF16)
    v_ref[...] = _dot(hn, w_ref[:, 2 * D_MODEL:]).astype(BF16)


def _qkv_call(x2, g, w_qkv, gq, gk, hs, tm=512):
    n = x2.shape[0]
    tok = pl.BlockSpec((tm, D_MODEL), lambda i: (i, 0))
    out = jax.ShapeDtypeStruct((n, D_MODEL), BF16)
    return pl.pallas_call(
        _qkv_kernel,
        grid=(n // tm,),
        in_specs=[tok, _const_spec((1, D_MODEL)), _const_spec((D_MODEL, 3 * D_MODEL)),
                  _const_spec((1, D_MODEL)), _const_spec((1, D_MODEL)), _const_spec((MXU_W, MXU_W))],
        out_specs=[tok, tok, tok],
        out_shape=[out, out, out],
        compiler_params=_params("parallel"),
        name="na_qkv",
    )(x2, g, w_qkv, gq, gk, hs)


def _na_kernel(q_ref, k_ref, v_ref, tbl_ref, o_ref, *, rows):
    kh = min(WIN_H, rows)
    lane = lax.broadcasted_iota(jnp.int32, (GRID_W, LANES), 1)
    head_mask = [lane < HEAD_DIM, lane >= HEAD_DIM]

    def rows_body(rb, carry):
        st = []
        for ri in range(NA_ROWS_PER_STEP):
            r = rb * NA_ROWS_PER_STEP + ri
            rs = jnp.clip(r - kh // 2, 0, rows - kh)
            qsl = pl.ds(pl.multiple_of(r * GRID_W, GRID_W), GRID_W)
            ksl = pl.ds(pl.multiple_of(rs * GRID_W, GRID_W), kh * GRID_W)
            q = q_ref[0, qsl, :]
            zero = jnp.zeros_like(q)
            qs = jnp.concatenate([jnp.where(head_mask[0], q, zero), jnp.where(head_mask[1], q, zero)], axis=0)
            st.append(dict(dr0=rs - r + WIN_H - 1, ksl=ksl, qsl=qsl, s=_dot_nt(qs, k_ref[0, ksl, :])))
        for e in st:
            bias = jnp.concatenate([tbl_ref[0, e["dr0"] + 2 * p] for p in range(kh // 2)], axis=1)
            s = e.pop("s") + bias
            m = jnp.max(s, axis=1, keepdims=True)
            p = jnp.exp(s - m)
            e["l"] = jnp.sum(p, axis=1, keepdims=True)
            e["p"] = p.astype(BF16)
        for e in st:
            e["o"] = _dot(e.pop("p"), v_ref[0, e["ksl"], :]) / e["l"]
        for e in st:
            o = e["o"]
            o_ref[0, e["qsl"], :] = jnp.where(head_mask[0], o[:GRID_W], o[GRID_W:]).astype(BF16)
        return carry

    lax.fori_loop(0, rows // NA_ROWS_PER_STEP, rows_body, 0)


def _na_call(q, k, v, tbl):
    b, t, _ = q.shape
    rows = t // GRID_W
    assert rows >= WIN_H and WIN_H % 2 == 0 and rows % NA_ROWS_PER_STEP == 0
    blk = pl.BlockSpec((1, t, LANES), lambda bi, hi: (bi, 0, hi))
    return pl.pallas_call(
        functools.partial(_na_kernel, rows=rows),
        grid=(b, N_HEAD_TILES),
        in_specs=[blk, blk, blk,
                  pl.BlockSpec((1, RPB_H - 1, HEADS_PER_TILE * GRID_W, LANES), lambda bi, hi: (hi, 0, 0, 0))],
        out_specs=blk,
        out_shape=jax.ShapeDtypeStruct((b, t, D_MODEL), BF16),
        compiler_params=_params("parallel", "parallel"),
        name="na_attn",
    )(q, k, v, tbl)


def _na_bias_table(rpb):
    c = np.arange(GRID_W)[:, None]
    kc = np.arange(GRID_W)[None, :]
    cs = np.clip(c - WIN_W // 2, 0, GRID_W - WIN_W)
    ok = (kc >= cs) & (kc < cs + WIN_W)
    d_col = np.clip(kc - c + WIN_W - 1, 0, RPB_W - 1)
    base = jnp.where(ok[None, None], rpb.astype(F32)[:, :, d_col], NEG_INF)
    pairs = jnp.concatenate([base[:, :-1], base[:, 1:]], axis=-1)
    pairs = pairs.reshape(N_HEAD_TILES, HEADS_PER_TILE, RPB_H - 1, GRID_W, LANES)
    return jnp.transpose(pairs, (0, 2, 1, 3, 4)).reshape(N_HEAD_TILES, RPB_H - 1, HEADS_PER_TILE * GRID_W, LANES)


def _shift_rows(cur, prev_row, next_row, tm):
    rid = lax.broadcasted_iota(jnp.int32, (F32_SUBLANES, 1), 0)
    down = pltpu.roll(cur, 1, axis=0)
    up = pltpu.roll(cur, tm - 1, axis=0)
    prev = jnp.concatenate([jnp.where(rid == 0, prev_row, down[:F32_SUBLANES]), down[F32_SUBLANES:]], axis=0)
    nxt = jnp.concatenate([up[:tm - F32_SUBLANES],
                           jnp.where(rid == F32_SUBLANES - 1, next_row, up[tm - F32_SUBLANES:])], axis=0)
    return prev, nxt


def _ext(refs3):
    return jnp.concatenate([r[...] for r in refs3], axis=0)


def _pre_phases(pre, blocks, consts):
    st = {}
    if pre == "proj":
        a3, x3 = blocks
        (wo_ref,) = consts

        def last():
            return _ext(x3) + _dot(_ext(a3), wo_ref[...])

        return [lambda: None, lambda: None, last]

    yf3, yb3, bonus3, gate3, x3 = blocks
    lnw_ref, lnb_ref, hs_ref, wo_ref = consts
    tiles = [slice(t * MXU_W, (t + 1) * MXU_W) for t in range(N_MXU_TILES)]

    def mean():
        st["y"] = _ext(yf3) + _ext(yb3)
        sums = []
        for sl in tiles:
            hi, lo = _split2(st["y"][:, sl])
            sums.append(_head_sum(hi, hs_ref[...]) + _head_sum(lo, hs_ref[...]))
        st["sum"] = sums

    def var():
        st["dlt"] = [st["y"][:, sl] - sm * (1.0 / HEAD_DIM) for sl, sm in zip(tiles, st["sum"])]
        st["sq"] = [_head_sum((d * d).astype(BF16), hs_ref[...]) for d in st["dlt"]]

    def last():
        yn = jnp.concatenate([d * lax.rsqrt(sq * (1.0 / HEAD_DIM) + GN_EPS) for d, sq in zip(st["dlt"], st["sq"])],
                             axis=1)
        yn = yn * lnw_ref[...] + lnb_ref[...]
        zz = ((yn + _ext(bonus3).astype(F32)) * _ext(gate3).astype(F32)).astype(BF16)
        return _ext(x3) + _dot(zz, wo_ref[...])

    return [mean, var, last]


PRE_AFTER_CHUNK = (1, 3, 6)


def _ffn_kernel(*refs, tm, seq, pre):
    n_tiled, n_const = (2, 1) if pre == "proj" else (5, 4)
    it = iter(refs)

    def take(k):
        return [next(it) for _ in range(k)]

    first_blocks = [take(3) for _ in range(n_tiled)]
    next_blocks = [take(3) for _ in range(n_tiled)]
    consts = take(n_const)
    g_ref, wg_ref, wu_ref, cw_ref, cb_ref, wd_ref, o_ref, xs_ref = take(8)

    @pl.when(pl.program_id(0) == 0)
    def _():
        phases = _pre_phases(pre, first_blocks, consts)
        phases[0]()
        phases[1]()
        xs_ref[...] = phases[2]()

    t0 = pl.program_id(0) * tm
    first = (t0 % seq) == 0
    last = ((t0 + tm) % seq) == 0
    x_ext = xs_ref[...]
    rid = lax.broadcasted_iota(jnp.int32, (tm + 2 * HALO, 1), 0)
    outside = jnp.logical_or(jnp.logical_and(first, rid < HALO), jnp.logical_and(last, rid >= tm + HALO))
    h_ext = jnp.where(outside, 0.0, _rms_rows(x_ext, g_ref[...]))
    s = F32_SUBLANES
    ext = tm + 2 * s
    hn_ext = h_ext[HALO - s:HALO + tm + s].astype(BF16)
    hn = h_ext[HALO:HALO + tm].astype(BF16)
    cw = 0.5 * cw_ref[...]
    cb = 0.5 * cb_ref[...]

    def gate_up(c):
        cs = slice(c * FF_CHUNK, (c + 1) * FF_CHUNK)
        return _dot(hn_ext, wg_ref[:, cs]), _dot(hn, wu_ref[:, cs])

    next_phases = _pre_phases(pre, next_blocks, consts)
    x_next = None
    acc = x_ext[HALO:HALO + tm]
    nxt = gate_up(0)
    for c in range(D_FF // FF_CHUNK):
        cs = slice(c * FF_CHUNK, (c + 1) * FF_CHUNK)
        ge, up = nxt
        if c + 1 < D_FF // FF_CHUNK:
            nxt = gate_up(c + 1)
        gprev = pltpu.roll(ge, 1, axis=0)[s:s + tm]
        gnext = pltpu.roll(ge, ext - 1, axis=0)[s:s + tm]
        h = gprev * cw[0:1, cs] + ge[s:s + tm] * cw[1:2, cs] + gnext * cw[2:3, cs] + cb[:, cs]
        act = (h * (jnp.tanh(h) + 1.0) * up).astype(BF16)
        acc = acc + _dot(act, wd_ref[cs, :])
        if c in PRE_AFTER_CHUNK:
            x_next = next_phases[PRE_AFTER_CHUNK.index(c)]()
    o_ref[...] = acc
    xs_ref[...] = x_next


def _halo_specs(tm, n, width, tile_of_step, single=False):
    hb = tm // HALO
    nhb = n // HALO
    mode = dict(pipeline_mode=pl.Buffered(1)) if single else {}
    return [pl.BlockSpec((HALO, width), lambda i: (jnp.maximum(tile_of_step(i) * hb - 1, 0), 0), **mode),
            pl.BlockSpec((tm, width), lambda i: (tile_of_step(i), 0), **mode),
            pl.BlockSpec((HALO, width), lambda i: (jnp.minimum((tile_of_step(i) + 1) * hb, nhb - 1), 0), **mode)]


def _ffn_call(x2, g, f, seq, pre, pre_args, tm=256):
    n = x2.shape[0]
    ntiles = n // tm
    if pre == "proj":
        a, w_o = pre_args
        tiled, consts = [a, x2], [w_o]
    else:
        yf, yb, bonus, gate, ln_w, ln_b, hs, w_o = pre_args
        tiled, consts = [yf, yb, bonus, gate, x2], [ln_w, ln_b, hs, w_o]
    consts = consts + [g, f["w_gate"], f["w_up"], f["conv_w"], f["conv_b"], f["w_down"]]
    args, specs = [], []
    for tile_of_step, single in ((lambda i: i * 0, True), (lambda i: jnp.minimum(i + 1, ntiles - 1), False)):
        for a in tiled:
            args.extend([a, a, a])
            specs.extend(_halo_specs(tm, n, a.shape[1], tile_of_step, single))
    for a in consts:
        args.append(a)
        specs.append(_const_spec(a.shape))
    return pl.pallas_call(
        functools.partial(_ffn_kernel, tm=tm, seq=seq, pre=pre),
        grid=(ntiles,),
        in_specs=specs,
        out_specs=pl.BlockSpec((tm, D_MODEL), lambda i: (i, 0)),
        out_shape=jax.ShapeDtypeStruct((n, D_MODEL), F32),
        scratch_shapes=[pltpu.VMEM((tm + 2 * HALO, D_MODEL), F32)],
        compiler_params=_params("arbitrary"),
        name="ffn_" + pre,
    )(*args)


def _rwkv_in_kernel(*refs, tm, seq, has_vres):
    (x_ref, xp_ref, xn_ref, g_ref, mu_ref, wr_ref, wk_ref, wv_ref, w1_ref, w2_ref, a1_ref, a2_ref,
     g1_ref, g2_ref, w0_ref, a0_ref, kk_ref, ka_ref, rk_ref, hs_ref) = refs[:20]
    refs = refs[20:]
    if has_vres:
        vf_ref, v0_ref, v1_ref, v2_ref = refs[:4]
        refs = refs[4:]
    r_out, v_out, aa_out, g_out, bonus_out, lw_out, kd_out, bb_out = refs

    t0 = pl.program_id(0) * tm
    first = (t0 % seq) == 0
    last = ((t0 + tm) % seq) == 0
    gn = g_ref[...]
    h = _rms_rows(x_ref[...], gn)
    prev_row = jnp.where(first, 0.0, _rms_rows(xp_ref[...], gn)[F32_SUBLANES - 1:F32_SUBLANES, :])
    next_row = jnp.where(last, 0.0, _rms_rows(xn_ref[...], gn)[0:1, :])
    prev, nxt = _shift_rows(h, prev_row, next_row, tm)
    xx = 0.5 * (prev + nxt) - h
    mu = mu_ref[...]

    def mix(i):
        return (h + xx * mu[i:i + 1, :]).astype(BF16)

    hs = hs_ref[...]
    tiles = [slice(t * MXU_W, (t + 1) * MXU_W) for t in range(N_MXU_TILES)]

    la = _dot(_dot(mix(4), a1_ref[...]).astype(BF16), a2_ref[...])
    lwl = _dot(jnp.tanh(_dot(mix(1), w1_ref[...])).astype(BF16), w2_ref[...])
    k = _dot(mix(2), wk_ref[...])
    kk = k * kk_ref[...]
    kkn = jnp.concatenate(
        [kk[:, sl] * jnp.minimum(lax.rsqrt(_head_sum((kk[:, sl] * kk[:, sl]).astype(BF16), hs)), 1e12)
         for sl in tiles], axis=1)
    aa_out[...] = (-kkn).astype(BF16)
    ka1, ka2 = ka_ref[0:1, :], ka_ref[1:2, :]
    half_decay = -0.5 * math.exp(-0.5)
    kd_sum = None
    for z in range(2):
        zs = slice(z * D_MODEL, (z + 1) * D_MODEL)
        lw_out[z] = half_decay * jnp.tanh(w0_ref[:, zs] + lwl[:, zs]) + half_decay
        th = jnp.tanh(a0_ref[:, zs] + la[:, zs])
        a = 0.5 * th + 0.5
        kd = k * (ka1 + ka2 * th)
        kd_out[z] = kd.astype(BF16)
        bb_out[z] = (kkn * a).astype(BF16)
        kd_sum = kd if kd_sum is None else kd_sum + kd

    r = _dot(mix(0), wr_ref[...])
    r_out[...] = r.astype(BF16)
    rkk = r * kd_sum * rk_ref[...]
    rk_heads = []
    for sl in tiles:
        hi, lo = _split2(rkk[:, sl])
        rk_heads.append(_head_sum(hi, hs) + _head_sum(lo, hs))
    xv = mix(3)
    v = _dot(xv, wv_ref[...])
    if has_vres:
        vl = _dot(_dot(xv, v1_ref[...]).astype(BF16), v2_ref[...])
        v = v + (vf_ref[...] - v) * _sigmoid(v0_ref[...] + vl)
    v_out[...] = v
    bonus_out[...] = (jnp.concatenate(rk_heads, axis=1) * v).astype(BF16)
    g_out[...] = _dot(_sigmoid(_dot(mix(5), g1_ref[...])).astype(BF16), g2_ref[...]).astype(BF16)


def _rwkv_in_call(x2, seq, p, v_first, tm=256):
    n = x2.shape[0]
    has_vres = v_first is not None
    hb = tm // F32_SUBLANES
    nhb = n // F32_SUBLANES
    tok = pl.BlockSpec((tm, D_MODEL), lambda i: (i, 0))
    tok2 = pl.BlockSpec((2, tm, D_MODEL), lambda i: (0, i, 0))
    halo_p = pl.BlockSpec((F32_SUBLANES, D_MODEL), lambda i: (jnp.maximum(i * hb - 1, 0), 0))
    halo_n = pl.BlockSpec((F32_SUBLANES, D_MODEL), lambda i: (jnp.minimum((i + 1) * hb, nhb - 1), 0))
    args = [x2, x2, x2, p["g"], p["mu"], p["w_r"], p["w_k"], p["w_v"], p["w1"], p["w2"], p["a1"], p["a2"],
            p["g1"], p["g2"], p["w0"], p["a0"], p["k_k"], p["k_a"], p["r_k"], p["hs"]]
    specs = [tok, halo_p, halo_n] + [_const_spec(a.shape) for a in args[3:]]
    if has_vres:
        extra = [v_first, p["v0"], p["v1"], p["v2"]]
        args += extra
        specs += [tok] + [_const_spec(a.shape) for a in extra[1:]]
    def one(dt):
        return jax.ShapeDtypeStruct((n, D_MODEL), dt)

    def two(dt):
        return jax.ShapeDtypeStruct((2, n, D_MODEL), dt)

    return pl.pallas_call(
        functools.partial(_rwkv_in_kernel, tm=tm, seq=seq, has_vres=has_vres),
        grid=(n // tm,),
        in_specs=specs,
        out_specs=[tok] * 5 + [tok2] * 3,
        out_shape=[one(BF16), one(F32), one(BF16), one(BF16), one(BF16), two(F32), two(BF16), two(BF16)],
        compiler_params=_params("parallel"),
        name="rwkv_in",
    )(*args)


def _wkv_kernel(rf_ref, vf_ref, af_ref, rb_ref, vb_ref, ab_ref, lwf_ref, kf_ref, bf_ref, lwb_ref, kb_ref, bb_ref,
                yf_ref, yb_ref, s_ref, *, tb, nt):
    assert CHUNK == HEAD_DIM
    nchunk = tb // CHUNK
    c1 = CHUNK

    @pl.when(pl.program_id(2) == 0)
    def _():
        s_ref[...] = jnp.zeros_like(s_ref)

    lane = lax.broadcasted_iota(jnp.int32, (c1, LANES), 1)
    d = lax.broadcasted_iota(jnp.int32, (c1, LANES), 0) - (lane & (c1 - 1))
    rid = lax.broadcasted_iota(jnp.int32, (c1, 1), 0)
    eye = (d == 0).astype(F32)
    m0 = lane < HEAD_DIM
    m0w = jnp.concatenate([m0, m0], axis=1)
    dirs = (
        (rf_ref, vf_ref, af_ref, lwf_ref, kf_ref, bf_ref, yf_ref, d > 0, d >= 0, False),
        (rb_ref, vb_ref, ab_ref, lwb_ref, kb_ref, bb_ref, yb_ref, d < 0, d <= 0, True),
    )

    def running_sum(x, backward):
        sh = 1
        while sh < c1:
            if backward:
                x = x + jnp.where(rid < c1 - sh, pltpu.roll(x, c1 - sh, axis=0), 0.0)
            else:
                x = x + jnp.where(rid >= sh, pltpu.roll(x, sh, axis=0), 0.0)
            sh *= 2
        return x

    def stack(x, mask=m0):
        zero = jnp.zeros_like(x)
        return jnp.concatenate([jnp.where(mask, x, zero), jnp.where(mask, zero, x)], axis=0)

    def chunk_body(ci, carry):
        st = []
        for z, (r_ref, v_ref, a_ref, lw_ref, k_ref, b_ref, y_ref, strict, incl, backward) in enumerate(dirs):
            c = ci if z == 0 else nchunk - 1 - ci
            sl = pl.ds(pl.multiple_of(c * CHUNK, CHUNK), CHUNK)
            lw_all = lw_ref[0, 0, sl, :]
            cl_all = running_sum(lw_all, backward)
            for t in range(nt):
                ls = slice(t * LANES, (t + 1) * LANES)
                lw = lw_all[:, ls]
                cl = cl_all[:, ls]
                e_in = jnp.exp(cl)
                e_ex = jnp.exp(cl - lw)
                e_neg = jnp.exp(-cl)
                q = dict(zi=z, t=t, sl=sl, ls=ls, y_ref=y_ref, strict=strict, incl=incl)
                tot = jnp.sum(lw, axis=0, keepdims=True)
                e_out = jnp.exp(tot - cl)
                bb, kd = b_ref[0, 0, sl, ls].astype(F32), k_ref[0, 0, sl, ls].astype(F32)
                q["wcol"] = jnp.broadcast_to(jnp.exp(tot), (HEADS_PER_TILE * c1, LANES)).T
                q["ar"] = jnp.concatenate([a_ref[0, sl, ls].astype(F32) * e_ex, r_ref[0, sl, ls].astype(F32) * e_in],
                                          axis=0).astype(BF16)
                q["bk"] = jnp.concatenate([stack((bb * e_neg).astype(BF16)), stack((kd * e_neg).astype(BF16))], axis=0)
                q["bk_t"] = jnp.concatenate([stack(bb * e_out).T, stack(kd * e_out).T], axis=1).astype(BF16)
                q["vs"] = stack(v_ref[0, sl, ls].astype(BF16))
                st.append(q)
        for q in st:
            q["g"] = _dot_nt(q["ar"], q["bk"])
        for q in st:
            g = q.pop("g")
            a_ab = jnp.where(q["strict"], g[:c1, :LANES], 0.0)
            q["a_ak"] = jnp.where(q["strict"], g[:c1, LANES:], 0.0).astype(BF16)
            q["a_r"] = jnp.concatenate([jnp.where(q["incl"], g[c1:, :LANES], 0.0),
                                        jnp.where(q["incl"], g[c1:, LANES:], 0.0)], axis=1).astype(BF16)
            q["tm"] = eye + a_ab
            q["pw"] = a_ab.astype(BF16)
        levels = CHUNK.bit_length() - 1
        for q in st:
            q["pw"] = _dot(q["pw"], stack(q["pw"])).astype(BF16)
        for _ in range(levels - 2):
            for q in st:
                pt = jnp.concatenate([q["pw"], q["tm"].astype(BF16)], axis=1)
                q["z"] = _dot(q["pw"], stack(pt, m0w))
            for q in st:
                z = q.pop("z")
                q["pw"] = z[:, :LANES].astype(BF16)
                q["tm"] = q["tm"] + z[:, LANES:]
        for q in st:
            q["tm"] = q["tm"] + _dot(q["pw"], stack(q["tm"].astype(BF16)))
        for q in st:
            q["s"] = s_ref[q["zi"], q["t"]]
            xy = _dot(q["ar"], q["s"].astype(BF16))
            q["y_s"] = xy[c1:]
            q["x"] = xy[:c1] + _dot(q["a_ak"], q["vs"])
        for q in st:
            q["u"] = stack(_dot(q["tm"].astype(BF16), stack(q["x"].astype(BF16))).astype(BF16))
        for q in st:
            uv = jnp.concatenate([q["u"], q["vs"]], axis=0)
            z = _dot(jnp.concatenate([q["a_r"], q["bk_t"]], axis=0), uv)
            q["y_ref"][0, q["sl"], q["ls"]] = q["y_s"] + z[:c1]
            s_ref[q["zi"], q["t"]] = q["s"] * q["wcol"] + z[c1:]
        return carry

    lax.fori_loop(0, nchunk, chunk_body, 0, unroll=4)


def _wkv_call(r, v, aa, lw, kd, bb, tb=512, nt=8):
    b, t, _ = r.shape
    tb = min(tb, t)
    ntb = t // tb
    w = nt * LANES
    fwd = pl.BlockSpec((1, tb, w), lambda bi, hi, j: (bi, j, hi))
    bwd = pl.BlockSpec((1, tb, w), lambda bi, hi, j: (bi, ntb - 1 - j, hi))
    fwd_d = pl.BlockSpec((1, 1, tb, w), lambda bi, hi, j: (0, bi, j, hi))
    bwd_d = pl.BlockSpec((1, 1, tb, w), lambda bi, hi, j: (1, bi, ntb - 1 - j, hi))
    out = jax.ShapeDtypeStruct((b, t, D_MODEL), F32)
    return pl.pallas_call(
        functools.partial(_wkv_kernel, tb=tb, nt=nt),
        grid=(b, N_HEAD_TILES // nt, ntb),
        in_specs=[fwd, fwd, fwd, bwd, bwd, bwd, fwd_d, fwd_d, fwd_d, bwd_d, bwd_d, bwd_d],
        out_specs=[fwd, bwd],
        out_shape=[out, out],
        scratch_shapes=[pltpu.VMEM((2, nt, LANES, LANES), F32)],
        compiler_params=_params("parallel", "parallel", "arbitrary"),
        name="wkv",
    )(r, v, aa, r, v, aa, lw, kd, bb, lw, kd, bb)


def _row(v):
    return v.reshape(1, -1).astype(F32)


def _pad_to(a, axis, size):
    pad = [(0, 0)] * a.ndim
    pad[axis] = (0, size - a.shape[axis])
    return jnp.pad(a, pad)


def _block_diag2(m):
    l, d = m.shape[1], m.shape[2]
    zero = jnp.zeros((l, d), m.dtype)
    return jnp.concatenate([jnp.concatenate([m[0], zero], axis=1), jnp.concatenate([zero, m[1]], axis=1)], axis=0)


def _round_up(n, m):
    return (n + m - 1) // m * m


def _prepare(w):
    hs = np.kron(np.eye(MXU_W // HEAD_DIM), np.ones((HEAD_DIM, HEAD_DIM)))
    hs = jnp.asarray(hs, BF16)
    na = []
    for li in range(w["na_w_qkv"].shape[0]):
        na.append(dict(
            w_qkv=w["na_w_qkv"][li].astype(BF16),
            w_o=w["na_w_o"][li].astype(BF16),
            gq=_row(jnp.tile(w["na_q_gain"][li], N_HEADS)) * (HEAD_DIM ** -0.5),
            gk=_row(jnp.tile(w["na_k_gain"][li], N_HEADS)),
            tbl=_na_bias_table(w["na_rpb"][li]),
        ))
    rw = []
    for li in range(w["rw_w_r"].shape[0]):
        lg = _round_up(w["rw_g1"].shape[-1], LANES)
        p = dict(
            mu=w["rw_mu"][li].astype(F32),
            w_r=w["rw_w_r"][li].astype(BF16), w_k=w["rw_w_k"][li].astype(BF16),
            w_v=w["rw_w_v"][li].astype(BF16), w_o=w["rw_w_o"][li].astype(BF16),
            w1=jnp.concatenate([w["rw_w1"][li, 0], w["rw_w1"][li, 1]], axis=1).astype(BF16),
            w2=(0.5 * _block_diag2(w["rw_w2"][li])).astype(BF16),
            a1=jnp.concatenate([w["rw_a1"][li, 0], w["rw_a1"][li, 1]], axis=1).astype(BF16),
            a2=(0.5 * _block_diag2(w["rw_a2"][li])).astype(BF16),
            g1=_pad_to(w["rw_g1"][li], 1, lg).astype(BF16),
            g2=_pad_to(w["rw_g2"][li], 0, lg).astype(BF16),
            w0=0.5 * _row(w["rw_w0"][li]), a0=0.5 * _row(w["rw_a0"][li]),
            k_k=_row(w["rw_k_k"][li]), r_k=_row(w["rw_r_k"][li]),
            k_a=jnp.concatenate([1.0 - 0.5 * _row(w["rw_k_a"][li]), 0.5 * _row(w["rw_k_a"][li])], axis=0),
            ln_w=_row(w["rw_ln_w"][li]), ln_b=_row(w["rw_ln_b"][li]),
            hs=hs,
        )
        if li >= 1:
            lv = _round_up(w["rw_v1"].shape[-1], LANES)
            p.update(v0=_row(w["rw_v0"][li - 1]),
                     v1=_pad_to(w["rw_v1"][li - 1], 1, lv).astype(BF16),
                     v2=_pad_to(w["rw_v2"][li - 1], 0, lv).astype(BF16))
        rw.append(p)
    ffn = []
    for i in range(DEPTH):
        ffn.append(dict(
            w_gate=w["ffn_w_gate"][i].astype(BF16), w_up=w["ffn_w_up"][i].astype(BF16),
            conv_w=w["ffn_conv_w"][i].astype(F32), conv_b=_row(w["ffn_conv_b"][i]),
            w_down=w["ffn_w_down"][i].astype(BF16),
        ))
    return dict(na=na, rw=rw, ffn=ffn, hs=hs,
                norm_mix=w["norm_mix"].astype(F32), norm_ffn=w["norm_ffn"].astype(F32))


def _trunk(x, p):
    b, t, d = x.shape
    n = b * t
    x2 = x.reshape(n, d)
    v_first = None
    for i in range(DEPTH):
        li = i // 2
        g_mix = _row(p["norm_mix"][i])
        if i % 2 == 0:
            a = p["na"][li]
            q, k, v = _qkv_call(x2, g_mix, a["w_qkv"], a["gq"], a["gk"], p["hs"])
            o = _na_call(q.reshape(b, t, d), k.reshape(b, t, d), v.reshape(b, t, d), a["tbl"])
            pre, pre_args = "proj", (o.reshape(n, d), a["w_o"])
        else:
            rp = dict(p["rw"][li], g=g_mix)
            r, v, aa, g, bonus, lw, kd, bb = _rwkv_in_call(x2, t, rp, v_first)
            if v_first is None:
                v_first = v
            yf, yb = _wkv_call(r.reshape(b, t, d), v.reshape(b, t, d), aa.reshape(b, t, d),
                               lw.reshape(2, b, t, d), kd.reshape(2, b, t, d), bb.reshape(2, b, t, d))
            pre, pre_args = "rwkv", (yf.reshape(n, d), yb.reshape(n, d), bonus, g, rp["ln_w"], rp["ln_b"], p["hs"],
                                     rp["w_o"])
        x2 = _ffn_call(x2, _row(p["norm_ffn"][i]), p["ffn"][i], t, pre, pre_args)
    return x2.reshape(b, t, d)


def kernel(x_prompt, x_sample, norm_mix, norm_ffn, na_w_qkv, na_w_o, na_q_gain, na_k_gain, na_rpb, rw_mu, rw_w_r, rw_w_k, rw_w_v, rw_w_o, rw_w0, rw_w1, rw_w2, rw_a0, rw_a1, rw_a2, rw_v0, rw_v1, rw_v2, rw_g1, rw_g2, rw_k_k, rw_k_a, rw_r_k, rw_ln_w, rw_ln_b, ffn_w_gate, ffn_w_up, ffn_conv_w, ffn_conv_b, ffn_w_down):
    w = dict(norm_mix=norm_mix, norm_ffn=norm_ffn, na_w_qkv=na_w_qkv, na_w_o=na_w_o, na_q_gain=na_q_gain,
             na_k_gain=na_k_gain, na_rpb=na_rpb, rw_mu=rw_mu, rw_w_r=rw_w_r, rw_w_k=rw_w_k, rw_w_v=rw_w_v,
             rw_w_o=rw_w_o, rw_w0=rw_w0, rw_w1=rw_w1, rw_w2=rw_w2, rw_a0=rw_a0, rw_a1=rw_a1, rw_a2=rw_a2,
             rw_v0=rw_v0, rw_v1=rw_v1, rw_v2=rw_v2, rw_g1=rw_g1, rw_g2=rw_g2, rw_k_k=rw_k_k, rw_k_a=rw_k_a,
             rw_r_k=rw_r_k, rw_ln_w=rw_ln_w, rw_ln_b=rw_ln_b, ffn_w_gate=ffn_w_gate, ffn_w_up=ffn_w_up,
             ffn_conv_w=ffn_conv_w, ffn_conv_b=ffn_conv_b, ffn_w_down=ffn_w_down)
    p = _prepare(w)
    return (_trunk(x_prompt, p), _trunk(x_sample, p))
```
